```python
import math
import jax, jax.numpy as jnp
from jax import lax
import numpy as np

D_MODEL = 1024
BATCH = 16
SEQ = 2048
DEPTH = 1
DEC_BATCH = 2
DEC_SEQ = 16384
PAST_LEN = 128

N_ATTN_HEADS = 4
QK_DIM = 64
V_DIM = 2 * QK_DIM
ATTN_WIDTH = N_ATTN_HEADS * V_DIM
QK_WIDTH = N_ATTN_HEADS * 2 * QK_DIM
N_FOURIER_GROUPS = 4
FOURIER_GROUP_DIM = 128
FOURIER_WIDTH = N_FOURIER_GROUPS * FOURIER_GROUP_DIM
MIX_WIDTH = ATTN_WIDTH + FOURIER_WIDTH
IN_WIDTH = 2 * QK_WIDTH + ATTN_WIDTH + FOURIER_WIDTH
D_FF = -(-8 * D_MODEL // (3 * 256)) * 256
ROPE_THETA = 10000.0
Q_BLOCK = 128
LN_EPS = 1e-5
ALPHA = (2 * DEPTH) ** 0.25
BETA = (8 * DEPTH) ** -0.25
N_MOD = 6

kernel_name = 'hymba_fnet_diffattn_deepnorm_encoder'


def _layer_norm(x):
    xf = x.astype(jnp.float32)
    mu = xf.mean(-1, keepdims=True)
    var = jnp.square(xf - mu).mean(-1, keepdims=True)
    return ((xf - mu) * lax.rsqrt(var + LN_EPS)).astype(x.dtype)


def _rms_norm(x):
    xf = x.astype(jnp.float32)
    return (xf * lax.rsqrt(jnp.mean(xf * xf, -1, keepdims=True) + LN_EPS)).astype(x.dtype)


def _rope_tables(seq):
    inv = ROPE_THETA ** (-jnp.arange(0, QK_DIM, 2, dtype=jnp.float32) / QK_DIM)
    ang = jnp.arange(seq, dtype=jnp.float32)[:, None] * inv[None, :]
    return jnp.cos(ang), jnp.sin(ang)


def _apply_rope(x, cos, sin):
    c = cos[None, :, None, None, :]
    s = sin[None, :, None, None, :]
    x1, x2 = jnp.split(x.astype(jnp.float32), 2, axis=-1)
    return jnp.concatenate([x1 * c - x2 * s, x2 * c + x1 * s], axis=-1).astype(x.dtype)


def _diff_attention(q, k, v, lam):
    b, s = q.shape[0], q.shape[1]
    nb = s // Q_BLOCK
    scale = QK_DIM ** -0.5
    qb = q.reshape(b, nb, Q_BLOCK, N_ATTN_HEADS, 2, QK_DIM).transpose(1, 0, 2, 3, 4, 5)

    def block(qi):
        sc = jnp.einsum('bqhcd,bkhcd->bhcqk', qi, k, preferred_element_type=jnp.float32) * scale
        p = jax.nn.softmax(sc, axis=-1)
        w = p[:, :, 0] - lam * p[:, :, 1]
        return jnp.einsum('bhqk,bkhd->bqhd', w.astype(v.dtype), v)

    o = lax.map(block, qb)
    return o.transpose(1, 0, 2, 3, 4).reshape(b, s, N_ATTN_HEADS, V_DIM)


def _fourier_mix(u, w_f):
    b, s, _ = u.shape
    ug = u.reshape(b, s, N_FOURIER_GROUPS, FOURIER_GROUP_DIM).astype(jnp.float32)
    f = jnp.fft.fft2(ug, axes=(1, 3), norm='ortho').real.astype(u.dtype)
    return jnp.einsum('bsgc,gcd->bsgd', f, w_f).reshape(b, s, FOURIER_WIDTH)


def _layer(x, c, l, w_ada, b_ada, w_in, lambda_q1, lambda_k1, lambda_q2, lambda_k2, subln_g,
           w_fourier, w_out, ln1_g, ln1_b, w_gate, w_up, w_down, ln2_g, ln2_b):
    b, s, _ = x.shape
    lambda_init = 0.8 - 0.6 * math.exp(-0.3 * l)
    mod = jnp.einsum('bd,de->be', jax.nn.silu(c), w_ada[l]) + b_ada[l]
    sh1, sc1, g1, sh2, sc2, g2 = jnp.split(mod[:, None, :], N_MOD, axis=-1)

    h = _layer_norm(x) * (1 + sc1) + sh1
    proj = jnp.einsum('bsd,de->bse', h, w_in[l])
    q, k, v, u = jnp.split(proj, [QK_WIDTH, 2 * QK_WIDTH, 2 * QK_WIDTH + ATTN_WIDTH], axis=-1)
    cos, sin = _rope_tables(s)
    q = _apply_rope(q.reshape(b, s, N_ATTN_HEADS, 2, QK_DIM), cos, sin)
    k = _apply_rope(k.reshape(b, s, N_ATTN_HEADS, 2, QK_DIM), cos, sin)
    v = v.reshape(b, s, N_ATTN_HEADS, V_DIM)
    lam = (jnp.exp(jnp.sum(lambda_q1[l].astype(jnp.float32) * lambda_k1[l].astype(jnp.float32)))
           - jnp.exp(jnp.sum(lambda_q2[l].astype(jnp.float32) * lambda_k2[l].astype(jnp.float32)))
           + lambda_init)
    o = _diff_attention(q, k, v, lam)
    o = _rms_norm(o) * subln_g[l] * (1 - lambda_init)
    f = _fourier_mix(u, w_fourier[l])
    mixed = jnp.concatenate([o.reshape(b, s, ATTN_WIDTH), f], axis=-1)
    mix_out = jnp.einsum('bse,ed->bsd', mixed, w_out[l])
    x = _layer_norm(ALPHA * x + (1 + g1) * mix_out) * ln1_g[l] + ln1_b[l]

    h = _layer_norm(x) * (1 + sc2) + sh2
    a = jax.nn.silu(jnp.einsum('bsd,df->bsf', h, w_gate[l])) * jnp.einsum('bsd,df->bsf', h, w_up[l])
    ffn = jnp.einsum('bsf,fd->bsd', a, w_down[l])
    x = _layer_norm(ALPHA * x + (1 + g2) * ffn) * ln2_g[l] + ln2_b[l]
    return x


def setup_inputs(seed: int = 0) -> dict:
    key = jax.random.key(seed)
    ks = jax.random.split(key, 24)
    nrm = lambda k, shape, s: jax.random.normal(k, shape, jnp.float32) * s
    w_in = nrm(ks[4], (DEPTH, D_MODEL, IN_WIDTH), D_MODEL ** -0.5)
    v_scale = jnp.concatenate([jnp.ones((2 * QK_WIDTH,), jnp.float32),
                               jnp.full((ATTN_WIDTH,), BETA, jnp.float32),
                               jnp.ones((FOURIER_WIDTH,), jnp.float32)])
    w_in = w_in * v_scale
    return {
        'x_prompt': nrm(ks[0], (BATCH, SEQ, D_MODEL), 1.0),
        'x_sample': nrm(ks[1], (DEC_BATCH, DEC_SEQ, D_MODEL), 1.0),
        'c_prompt': nrm(ks[2], (BATCH, D_MODEL), 1.0),
        'c_sample': nrm(ks[3], (DEC_BATCH, D_MODEL), 1.0),
        'w_ada': nrm(ks[5], (DEPTH, D_MODEL, N_MOD * D_MODEL), 0.1 * D_MODEL ** -0.5),
        'b_ada': nrm(ks[6], (DEPTH, N_MOD * D_MODEL), 0.01),
        'w_in': w_in,
        'lambda_q1': nrm(ks[7], (DEPTH, QK_DIM), 0.1),
        'lambda_k1': nrm(ks[8], (DEPTH, QK_DIM), 0.1),
        'lambda_q2': nrm(ks[9], (DEPTH, QK_DIM), 0.1),
        'lambda_k2': nrm(ks[10], (DEPTH, QK_DIM), 0.1),
        'subln_g': 1.0 + nrm(ks[11], (DEPTH, V_DIM), 0.02),
        'w_fourier': nrm(ks[12], (DEPTH, N_FOURIER_GROUPS, FOURIER_GROUP_DIM, FOURIER_GROUP_DIM), FOURIER_GROUP_DIM ** -0.5),
        'w_out': nrm(ks[13], (DEPTH, MIX_WIDTH, D_MODEL), BETA * MIX_WIDTH ** -0.5),
        'ln1_g': 1.0 + nrm(ks[14], (DEPTH, D_MODEL), 0.02),
        'ln1_b': nrm(ks[15], (DEPTH, D_MODEL), 0.02),
        'w_gate': nrm(ks[16], (DEPTH, D_MODEL, D_FF), BETA * D_MODEL ** -0.5),
        'w_up': nrm(ks[17], (DEPTH, D_MODEL, D_FF), BETA * D_MODEL ** -0.5),
        'w_down': nrm(ks[18], (DEPTH, D_FF, D_MODEL), BETA * D_FF ** -0.5),
        'ln2_g': 1.0 + nrm(ks[19], (DEPTH, D_MODEL), 0.02),
        'ln2_b': nrm(ks[20], (DEPTH, D_MODEL), 0.02),
    }


def reference(x_prompt, x_sample, c_prompt, c_sample, w_ada, b_ada, w_in, lambda_q1, lambda_k1,
              lambda_q2, lambda_k2, subln_g, w_fourier, w_out, ln1_g, ln1_b, w_gate, w_up, w_down,
              ln2_g, ln2_b):
    y_prompt = x_prompt
    y_sample = x_sample
    for l in range(DEPTH):
        y_prompt = _layer(y_prompt, c_prompt, l, w_ada, b_ada, w_in, lambda_q1, lambda_k1, lambda_q2,
                          lambda_k2, subln_g, w_fourier, w_out, ln1_g, ln1_b, w_gate, w_up, w_down,
                          ln2_g, ln2_b)
        y_sample = _layer(y_sample, c_sample, l, w_ada, b_ada, w_in, lambda_q1, lambda_k1, lambda_q2,
                          lambda_k2, subln_g, w_fourier, w_out, ln1_g, ln1_b, w_gate, w_up, w_down,
                          ln2_g, ln2_b)
    return (y_prompt, y_sample)
```

```python
import functools
import math

import numpy as np
import jax
import jax.numpy as jnp
from jax import lax
from jax.experimental import pallas as pl
from jax.experimental.pallas import tpu as pltpu

D_MODEL = 1024
DEPTH = 1
N_ATTN_HEADS = 4
QK_DIM = 64
V_DIM = 2 * QK_DIM
ATTN_WIDTH = N_ATTN_HEADS * V_DIM
QK_WIDTH = N_ATTN_HEADS * 2 * QK_DIM
N_FOURIER_GROUPS = 4
FOURIER_GROUP_DIM = 128
FOURIER_WIDTH = N_FOURIER_GROUPS * FOURIER_GROUP_DIM
IN_WIDTH = 2 * QK_WIDTH + ATTN_WIDTH + FOURIER_WIDTH
D_FF = -(-8 * D_MODEL // (3 * 256)) * 256
ROPE_THETA = 10000.0
LN_EPS = 1e-5
ALPHA = (2 * DEPTH) ** 0.25
N_MOD = 6

LANES = 128
VMEM_LIMIT = 56 * 1024 * 1024
DIRECT_DFT_MAX_SEQ = 2048
DFT_STAGE1 = 128

F32 = jnp.float32
BF16 = jnp.bfloat16


def _params(n_grid_dims):
    return pltpu.CompilerParams(
        dimension_semantics=("arbitrary",) * n_grid_dims, vmem_limit_bytes=VMEM_LIMIT)


def _layer_norm(x):
    mu = jnp.mean(x, axis=-1, keepdims=True)
    xc = x - mu
    var = jnp.mean(xc * xc, axis=-1, keepdims=True)
    return xc * lax.rsqrt(var + LN_EPS)


def _silu(x):
    return x * (1.0 / (1.0 + jnp.exp(-x)))


def _ada_kernel(c_ref, w_ref, b_ref, o_ref):
    s = _silu(c_ref[...])
    o_ref[...] = jnp.dot(s, w_ref[...], preferred_element_type=F32,
                         precision=lax.Precision.HIGHEST) + b_ref[...]


def _ada(c, w, b, tn=1536):
    n_rows, d = c.shape
    n_out = w.shape[1]
    return pl.pallas_call(
        _ada_kernel,
        grid=(n_out // tn,),
        in_specs=[pl.BlockSpec((n_rows, d), lambda j: (0, 0)),
                  pl.BlockSpec((d, tn), lambda j: (0, j)),
                  pl.BlockSpec((1, tn), lambda j: (0, j))],
        out_specs=pl.BlockSpec((n_rows, tn), lambda j: (0, j)),
        out_shape=jax.ShapeDtypeStruct((n_rows, n_out), F32),
        compiler_params=_params(1),
        name="ada",
    )(c, w, b)


def _inproj_kernel(x_ref, mod_ref, w_ref, cos_ref, sin_ref, q_ref, k_ref, v_ref, u_ref):
    h = _layer_norm(x_ref[...]) * (1.0 + mod_ref[1:2, :]) + mod_ref[0:1, :]
    proj = jnp.dot(h.astype(BF16), w_ref[...], preferred_element_type=F32)
    cos = cos_ref[...]
    sin = sin_ref[...]
    lane = lax.broadcasted_iota(jnp.int32, cos.shape, 1)
    first_half = (lane % QK_DIM) < (QK_DIM // 2)
    for base, ref, scale in ((0, q_ref, QK_DIM ** -0.5), (QK_WIDTH, k_ref, 1.0)):
        for hd in range(N_ATTN_HEADS):
            xh = proj[:, base + hd * LANES: base + (hd + 1) * LANES]
            partner = jnp.where(first_half,
                                pltpu.roll(xh, LANES - QK_DIM // 2, 1),
                                pltpu.roll(xh, QK_DIM // 2, 1))
            r = xh * cos + partner * sin
            ref[:, hd * LANES:(hd + 1) * LANES] = (r * scale).astype(BF16)
    v_ref[...] = proj[:, 2 * QK_WIDTH: 2 * QK_WIDTH + ATTN_WIDTH].astype(BF16)
    u_ref[...] = proj[:, 2 * QK_WIDTH + ATTN_WIDTH:]


def _inproj(x, mod, w_in_bf16, cos_t, sin_t, tm=512):
    b, s, d = x.shape
    tm = min(tm, s)
    tok = lambda bi, i: (bi, i, 0)
    out_block = pl.BlockSpec((None, tm, QK_WIDTH), tok)
    return pl.pallas_call(
        _inproj_kernel,
        grid=(b, s // tm),
        in_specs=[pl.BlockSpec((None, tm, d), tok),
                  pl.BlockSpec((None, N_MOD, d), lambda bi, i: (bi, 0, 0)),
                  pl.BlockSpec((d, IN_WIDTH), lambda bi, i: (0, 0)),
                  pl.BlockSpec((tm, LANES), lambda bi, i: (i, 0)),
                  pl.BlockSpec((tm, LANES), lambda bi, i: (i, 0))],
        out_specs=[out_block, out_block, out_block, out_block],
        out_shape=[jax.ShapeDtypeStruct((b, s, QK_WIDTH), BF16),
                   jax.ShapeDtypeStruct((b, s, QK_WIDTH), BF16),
                   jax.ShapeDtypeStruct((b, s, ATTN_WIDTH), BF16),
                   jax.ShapeDtypeStruct((b, s, FOURIER_WIDTH), F32)],
        compiler_params=_params(2),
        name="inproj",
    )(x, mod, w_in_bf16, cos_t, sin_t)


def _attn_kernel(lam_ref, g_ref, q_ref, k_ref, v_ref, o_ref, *, tk, lambda_init):
    tq = q_ref.shape[0]
    n_chunks = k_ref.shape[0] // tk
    q = q_ref[...]
    lane = lax.broadcasted_iota(jnp.int32, q.shape, 1)
    zero = jnp.zeros_like(q)
    qq = jnp.concatenate([jnp.where(lane < QK_DIM, q, zero),
                          jnp.where(lane >= QK_DIM, q, zero)], axis=0)

    def body(j, carry):
        m, l, acc = carry
        start = pl.multiple_of(j * tk, tk)
        kc = k_ref[pl.ds(start, tk), :]
        vc = v_ref[pl.ds(start, tk), :]
        s = lax.dot_general(qq, kc, (((1,), (1,)), ((), ())), preferred_element_type=F32)
        m_new = jnp.maximum(m, jnp.max(s, axis=-1, keepdims=True))
        alpha = jnp.exp(m - m_new)
        p = jnp.exp(s - m_new)
        l = alpha * l + jnp.sum(p, axis=-1, keepdims=True)
        acc = alpha * acc + jnp.dot(p.astype(BF16), vc, preferred_element_type=F32)
        return m_new, l, acc

    m0 = jnp.full((2 * tq, 1), -jnp.inf, F32)
    l0 = jnp.zeros((2 * tq, 1), F32)
    acc0 = jnp.zeros((2 * tq, V_DIM), F32)
    _, l, acc = lax.fori_loop(0, n_chunks, body, (m0, l0, acc0))
    on = acc / l
    lam = (jnp.exp(jnp.sum(lam_ref[0:1, :] * lam_ref[1:2, :], axis=-1, keepdims=True))
           - jnp.exp(jnp.sum(lam_ref[2:3, :] * lam_ref[3:4, :], axis=-1, keepdims=True))
           + lambda_init)
    o = on[:tq] - lam * on[tq:]
    o = o * lax.rsqrt(jnp.mean(o * o, axis=-1, keepdims=True) + LN_EPS)
    o_ref[...] = (o * g_ref[...] * (1.0 - lambda_init)).astype(BF16)


def _attention(q, k, v, lam_vecs, subln_g, lambda_init, tq=256, tk=512):
    b, s, _ = q.shape
    tq = min(tq, s)
    tk = min(tk, s)
    kv_block = pl.BlockSpec((None, s, LANES), lambda bi, hd, i: (bi, 0, hd))
    q_block = pl.BlockSpec((None, tq, LANES), lambda bi, hd, i: (bi, i, hd))
    return pl.pallas_call(
        functools.partial(_attn_kernel, tk=tk, lambda_init=lambda_init),
        grid=(b, N_ATTN_HEADS, s // tq),
        in_specs=[pl.BlockSpec((4, QK_DIM), lambda bi, hd, i: (0, 0)),
                  pl.BlockSpec((1, V_DIM), lambda bi, hd, i: (0, 0)),
                  q_block, kv_block, kv_block],
        out_specs=q_block,
        out_shape=jax.ShapeDtypeStruct((b, s, ATTN_WIDTH), BF16),
        compiler_params=_params(3),
        name="attention",
    )(lam_vecs, subln_g, q, k, v)


def _dft_tables(n):
    idx = np.arange(n, dtype=np.int64)
    ang = (2.0 * np.pi / n) * ((idx[:, None] * idx[None, :]) % n)
    return np.cos(ang), np.sin(ang)


def _channel_table(seq):
    c, s = _dft_tables(FOURIER_GROUP_DIM)
    scale = 1.0 / math.sqrt(seq * FOURIER_GROUP_DIM)
    return jnp.asarray(np.concatenate([c, -s], axis=0) * scale, F32).astype(BF16)


def _fourier_direct_kernel(u_ref, cs_ref, ch_ref, wf_ref, o_ref):
    u = u_ref[...].astype(BF16)
    pr = jnp.dot(cs_ref[0], u, preferred_element_type=F32)
    qi = jnp.dot(cs_ref[1], u, preferred_element_type=F32)
    ch = ch_ref[...]
    for g in range(N_FOURIER_GROUPS):
        sl = slice(g * LANES, (g + 1) * LANES)
        y = jnp.concatenate([pr[:, sl], qi[:, sl]], axis=1).astype(BF16)
        f = jnp.dot(y, ch, preferred_element_type=F32)
        o_ref[:, sl] = jnp.dot(f.astype(BF16), wf_ref[g].astype(BF16),
                               preferred_element_type=F32).astype(o_ref.dtype)


def _fourier_direct(u, w_f, tt=512):
    b, s, _ = u.shape
    tt = min(tt, s)
    c, sn = _dft_tables(s)
    cs = jnp.asarray(np.stack([c, sn]), F32).astype(BF16)
    return pl.pallas_call(
        _fourier_direct_kernel,
        grid=(s // tt, b),
        in_specs=[pl.BlockSpec((None, s, FOURIER_WIDTH), lambda i, bi: (bi, 0, 0)),
                  pl.BlockSpec((2, tt, s), lambda i, bi: (0, i, 0)),
                  pl.BlockSpec((2 * FOURIER_GROUP_DIM, FOURIER_GROUP_DIM), lambda i, bi: (0, 0)),
                  pl.BlockSpec(w_f.shape, lambda i, bi: (0, 0, 0))],
        out_specs=pl.BlockSpec((None, tt, FOURIER_WIDTH), lambda i, bi: (bi, i, 0)),
        out_shape=jax.ShapeDtypeStruct((b, s, FOURIER_WIDTH), BF16),
        compiler_params=_params(2),
        name="fourier_direct",
    )(u, cs, _channel_table(s), w_f)


def _fourier_two_stage_kernel(u_ref, m1_ref, tw_ref, m2_ref, ch_ref, wf_ref, o_ref,
                              gr_ref, hr_ref, *, n1, n2):
    m1 = m1_ref[...]
    cb = tw_ref[0]
    sb = tw_ref[1]

    def stage1(s2, carry):
        cw, sw = carry
        us = u_ref[pl.ds(s2, n1, stride=n2), :].astype(BF16)
        gh = jnp.dot(m1, us, preferred_element_type=F32)
        g = gh[:n1]
        h = gh[n1:]
        row0 = pl.multiple_of(s2 * n1, n1)
        gr_ref[pl.ds(row0, n1), :] = g * cw - h * sw
        hr_ref[pl.ds(row0, n1), :] = g * sw + h * cw
        return cw * cb - sw * sb, sw * cb + cw * sb

    lax.fori_loop(0, n2, stage1, (jnp.ones_like(cb), jnp.zeros_like(sb)))

    m2 = m2_ref[...]
    ch = ch_ref[...]
    wf = wf_ref[...].astype(BF16)

    def stage2(t1, carry):
        x = jnp.concatenate([gr_ref[pl.ds(t1, n2, stride=n1), :],
                             hr_ref[pl.ds(t1, n2, stride=n1), :]], axis=0).astype(BF16)
        pq = jnp.dot(m2, x, preferred_element_type=F32)
        y = jnp.concatenate([pq[:n2], pq[n2:]], axis=1).astype(BF16)
        f = jnp.dot(y, ch, preferred_element_type=F32)
        o_ref[pl.ds(t1, n2, stride=n1), :] = jnp.dot(f.astype(BF16), wf,
                                                     preferred_element_type=F32)
        return carry

    lax.fori_loop(0, n1, stage2, 0)


def _fourier_two_stage(u, w_f, n1=DFT_STAGE1):
    b, s, _ = u.shape
    n2 = s // n1
    c1, s1 = _dft_tables(n1)
    c2, s2 = _dft_tables(n2)
    m1 = jnp.asarray(np.concatenate([c1, s1], axis=0), F32).astype(BF16)
    m2 = jnp.asarray(np.block([[c2, -s2], [s2, c2]]), F32).astype(BF16)
    ang = (2.0 * np.pi / s) * np.arange(n1, dtype=np.float64)
    tw = np.stack([np.broadcast_to(np.cos(ang)[:, None], (n1, LANES)),
                   np.broadcast_to(np.sin(ang)[:, None], (n1, LANES))])
    tw = jnp.asarray(tw, F32)
    grp = lambda bi, g: (bi, 0, g)
    return pl.pallas_call(
        functools.partial(_fourier_two_stage_kernel, n1=n1, n2=n2),
        grid=(b, N_FOURIER_GROUPS),
        in_specs=[pl.BlockSpec((None, s, LANES), grp, pipeline_mode=pl.Buffered(1)),
                  pl.BlockSpec(m1.shape, lambda bi, g: (0, 0)),
                  pl.BlockSpec(tw.shape, lambda bi, g: (0, 0, 0)),
                  pl.BlockSpec(m2.shape, lambda bi, g: (0, 0)),
                  pl.BlockSpec((2 * FOURIER_GROUP_DIM, FOURIER_GROUP_DIM), lambda bi, g: (0, 0)),
                  pl.BlockSpec((None, FOURIER_GROUP_DIM, FOURIER_GROUP_DIM), lambda bi, g: (g, 0, 0))],
        out_specs=pl.BlockSpec((None, s, LANES), grp),
        out_shape=jax.ShapeDtypeStruct((b, s, FOURIER_WIDTH), F32),
        scratch_shapes=[pltpu.VMEM((s, LANES), F32), pltpu.VMEM((s, LANES), F32)],
        compiler_params=_params(2),
        name="fourier_two_stage",
    )(u, m1, tw, m2, _channel_table(s), w_f)


def _fourier(u, w_f):
    if u.shape[1] <= DIRECT_DFT_MAX_SEQ:
        return _fourier_direct(u, w_f)
    return _fourier_two_stage(u, w_f)


def _ffn_kernel(x_ref, o_ref, f_ref, mod_ref, wo_ref, ln1g_ref, ln1b_ref,
                wg_ref, wu_ref, wd_ref, ln2g_ref, ln2b_ref, y_ref):
    mix = (jnp.dot(o_ref[...], wo_ref[:ATTN_WIDTH, :], preferred_element_type=F32)
           + jnp.dot(f_ref[...].astype(BF16), wo_ref[ATTN_WIDTH:, :], preferred_element_type=F32))
    g1 = mod_ref[2:3, :]
    sh2 = mod_ref[3:4, :]
    sc2 = mod_ref[4:5, :]
    g2 = mod_ref[5:6, :]
    x1 = _layer_norm(ALPHA * x_ref[...] + (1.0 + g1) * mix) * ln1g_ref[...] + ln1b_ref[...]
    h = (_layer_norm(x1) * (1.0 + sc2) + sh2).astype(BF16)
    gate = jnp.dot(h, wg_ref[...], preferred_element_type=F32)
    up = jnp.dot(h, wu_ref[...], preferred_element_type=F32)
    a = (_silu(gate) * up).astype(BF16)
    ffn = jnp.dot(a, wd_ref[...], preferred_element_type=F32)
    y_ref[...] = _layer_norm(ALPHA * x1 + (1.0 + g2) * ffn) * ln2g_ref[...] + ln2b_ref[...]


def _ffn(x, o, f, mod, wo, ln1g, ln1b, wg, wu, wd, ln2g, ln2b, tm=256):
    b, s, d = x.shape
    tm = min(tm, s)
    tok = lambda bi, i: (bi, i, 0)
    const = lambda bi, i: (0, 0)
    resident = lambda shape: pl.BlockSpec(shape, const, pipeline_mode=pl.Buffered(1))
    return pl.pallas_call(
        _ffn_kernel,
        grid=(b, s // tm),
        in_specs=[pl.BlockSpec((None, tm, d), tok),
                  pl.BlockSpec((None, tm, ATTN_WIDTH), tok),
                  pl.BlockSpec((None, tm, FOURIER_WIDTH), tok),
                  pl.BlockSpec((None, N_MOD, d), lambda bi, i: (bi, 0, 0)),
                  resident(wo.shape), resident((1, d)), resident((1, d)),
                  resident(wg.shape), resident(wu.shape), resident(wd.shape),
                  resident((1, d)), resident((1, d))],
        out_specs=pl.BlockSpec((None, tm, d), tok),
        out_shape=jax.ShapeDtypeStruct((b, s, d), F32),
        compiler_params=_params(2),
        name="ffn",
    )(x, o, f, mod, wo, ln1g, ln1b, wg, wu, wd, ln2g, ln2b)


def _rope_tables(seq):
    inv = ROPE_THETA ** (-jnp.arange(0, QK_DIM, 2, dtype=F32) / QK_DIM)
    ang = jnp.arange(seq, dtype=F32)[:, None] * inv[None, :]
    cos = jnp.cos(ang)
    sin = jnp.sin(ang)
    reps = LANES // (QK_DIM // 2)
    sign = jnp.tile(jnp.concatenate([-jnp.ones((QK_DIM // 2,), F32), jnp.ones((QK_DIM // 2,), F32)]),
                    LANES // QK_DIM)
    return jnp.tile(cos, (1, reps)), jnp.tile(sin, (1, reps)) * sign


def _layer(x, mod, l, w_in, lam_vecs, subln_g, w_fourier, w_out, ln1_g, ln1_b,
           w_gate, w_up, w_down, ln2_g, ln2_b):
    lambda_init = 0.8 - 0.6 * math.exp(-0.3 * l)
    cos_t, sin_t = _rope_tables(x.shape[1])
    q, k, v, u = _inproj(x, mod, w_in, cos_t, sin_t)
    o = _attention(q, k, v, lam_vecs, subln_g, lambda_init)
    f = _fourier(u, w_fourier)
    return _ffn(x, o, f, mod, w_out, ln1_g, ln1_b, w_gate, w_up, w_down, ln2_g, ln2_b)


def kernel(x_prompt, x_sample, c_prompt, c_sample, w_ada, b_ada, w_in, lambda_q1, lambda_k1, lambda_q2, lambda_k2, subln_g, w_fourier, w_out, ln1_g, ln1_b, w_gate, w_up, w_down, ln2_g, ln2_b):
    n_prompt = c_prompt.shape[0]
    c_all = jnp.concatenate([c_prompt, c_sample], axis=0)
    y_prompt, y_sample = x_prompt, x_sample
    for l in range(DEPTH):
        mod = _ada(c_all, w_ada[l], b_ada[l][None, :]).reshape(c_all.shape[0], N_MOD, D_MODEL)
        lam_vecs = jnp.stack([lambda_q1[l], lambda_k1[l], lambda_q2[l], lambda_k2[l]]).astype(F32)
        weights = (w_in[l].astype(BF16), lam_vecs, subln_g[l][None, :], w_fourier[l],
                   w_out[l].astype(BF16), ln1_g[l][None, :], ln1_b[l][None, :],
                   w_gate[l].astype(BF16), w_up[l].astype(BF16), w_down[l].astype(BF16),
                   ln2_g[l][None, :], ln2_b[l][None, :])
        y_prompt = _layer(y_prompt, mod[:n_prompt], l, *weights)
        y_sample = _layer(y_sample, mod[n_prompt:], l, *weights)
    return (y_prompt, y_sample)
```

```python
import functools
import math

import numpy as np
import jax
import jax.numpy as jnp
from jax import lax
from jax.experimental import pallas as pl
from jax.experimental.pallas import tpu as pltpu

D_MODEL = 1024
DEPTH = 1
N_ATTN_HEADS = 4
QK_DIM = 64
V_DIM = 2 * QK_DIM
ATTN_WIDTH = N_ATTN_HEADS * V_DIM
QK_WIDTH = N_ATTN_HEADS * 2 * QK_DIM
N_FOURIER_GROUPS = 4
FOURIER_GROUP_DIM = 128
FOURIER_WIDTH = N_FOURIER_GROUPS * FOURIER_GROUP_DIM
IN_WIDTH = 2 * QK_WIDTH + ATTN_WIDTH + FOURIER_WIDTH
D_FF = -(-8 * D_MODEL // (3 * 256)) * 256
ROPE_THETA = 10000.0
LN_EPS = 1e-5
ALPHA = (2 * DEPTH) ** 0.25
N_MOD = 6

LANES = 128
VMEM_LIMIT = 56 * 1024 * 1024
DIRECT_DFT_MAX_SEQ = 2048
DFT_STAGE1 = 128
KV_CHUNK = 512

F32 = jnp.float32
BF16 = jnp.bfloat16


def _params(n_grid_dims):
    return pltpu.CompilerParams(
        dimension_semantics=("arbitrary",) * n_grid_dims, vmem_limit_bytes=VMEM_LIMIT)


def _layer_norm(x):
    mu = jnp.mean(x, axis=-1, keepdims=True)
    xc = x - mu
    var = jnp.mean(xc * xc, axis=-1, keepdims=True)
    return xc * lax.rsqrt(var + LN_EPS)


def _silu(x):
    return x * (1.0 / (1.0 + jnp.exp(-x)))


def _ada_kernel(c_ref, w_ref, b_ref, o_ref):
    s = _silu(c_ref[...])
    o_ref[...] = jnp.dot(s, w_ref[...], preferred_element_type=F32,
                         precision=lax.Precision.HIGHEST) + b_ref[...]


def _ada(c, w, b, tn=1536):
    n_rows, d = c.shape
    n_out = w.shape[1]
    return pl.pallas_call(
        _ada_kernel,
        grid=(n_out // tn,),
        in_specs=[pl.BlockSpec((n_rows, d), lambda j: (0, 0)),
                  pl.BlockSpec((d, tn), lambda j: (0, j)),
                  pl.BlockSpec((1, tn), lambda j: (0, j))],
        out_specs=pl.BlockSpec((n_rows, tn), lambda j: (0, j)),
        out_shape=jax.ShapeDtypeStruct((n_rows, n_out), F32),
        compiler_params=_params(1),
        name="ada",
    )(c, w, b)


def _inproj_kernel(x_ref, mod_ref, wqv_ref, wku_ref, cos_ref, sin_ref, cost_ref, sint_ref,
                   qt_ref, k_ref, vt_ref, u_ref):
    h = (_layer_norm(x_ref[...]) * (1.0 + mod_ref[1:2, :]) + mod_ref[0:1, :]).astype(BF16)
    qvt = lax.dot_general(wqv_ref[...], h, (((1,), (1,)), ((), ())), preferred_element_type=F32)
    cost = cost_ref[...]
    sint = sint_ref[...]
    half = QK_DIM // 2
    scale = QK_DIM ** -0.5 * math.log2(math.e)
    for c in range(QK_WIDTH // QK_DIM):
        a = qvt[c * QK_DIM: c * QK_DIM + half, :]
        b = qvt[c * QK_DIM + half: (c + 1) * QK_DIM, :]
        qt_ref[c * QK_DIM: c * QK_DIM + half, :] = ((a * cost - b * sint) * scale).astype(BF16)
        qt_ref[c * QK_DIM + half: (c + 1) * QK_DIM, :] = ((b * cost + a * sint) * scale).astype(BF16)
    vt_ref[...] = qvt[QK_WIDTH:, :].astype(BF16)

    ku = jnp.dot(h, wku_ref[...], preferred_element_type=F32)
    cos = cos_ref[...]
    sin = sin_ref[...]
    lane = lax.broadcasted_iota(jnp.int32, cos.shape, 1)
    first_half = (lane % QK_DIM) < half
    for hd in range(N_ATTN_HEADS):
        xh = ku[:, hd * LANES:(hd + 1) * LANES]
        partner = jnp.where(first_half, pltpu.roll(xh, LANES - half, 1), pltpu.roll(xh, half, 1))
        k_ref[:, hd * LANES:(hd + 1) * LANES] = (xh * cos + partner * sin).astype(BF16)
    u_ref[...] = ku[:, QK_WIDTH:]


def _inproj(x, mod, w_qv_t, w_ku, rope, tm):
    b, s, d = x.shape
    cos_t, sin_t, cos_tt, sin_tt = rope
    tok = lambda bi, i: (bi, i, 0)
    const = lambda bi, i: (0, 0)
    return pl.pallas_call(
        _inproj_kernel,
        grid=(b, s // tm),
        in_specs=[pl.BlockSpec((None, tm, d), tok),
                  pl.BlockSpec((None, N_MOD, d), lambda bi, i: (bi, 0, 0)),
                  pl.BlockSpec(w_qv_t.shape, const),
                  pl.BlockSpec(w_ku.shape, const),
                  pl.BlockSpec((tm, LANES), lambda bi, i: (i, 0)),
                  pl.BlockSpec((tm, LANES), lambda bi, i: (i, 0)),
                  pl.BlockSpec((QK_DIM // 2, tm), lambda bi, i: (0, i)),
                  pl.BlockSpec((QK_DIM // 2, tm), lambda bi, i: (0, i))],
        out_specs=[pl.BlockSpec((None, QK_WIDTH, tm), lambda bi, i: (bi, 0, i)),
                   pl.BlockSpec((None, tm, QK_WIDTH), tok),
                   pl.BlockSpec((None, None, ATTN_WIDTH, tm), lambda bi, i: (bi, i, 0, 0)),
                   pl.BlockSpec((None, tm, FOURIER_WIDTH), tok)],
        out_shape=[jax.ShapeDtypeStruct((b, QK_WIDTH, s), BF16),
                   jax.ShapeDtypeStruct((b, s, QK_WIDTH), BF16),
                   jax.ShapeDtypeStruct((b, s // tm, ATTN_WIDTH, tm), BF16),
                   jax.ShapeDtypeStruct((b, s, FOURIER_WIDTH), F32)],
        compiler_params=_params(2),
        name="inproj",
    )(x, mod, w_qv_t, w_ku, cos_t, sin_t, cos_tt, sin_tt)


def _attn_kernel(lam_ref, g_ref, qt_ref, k_ref, vt_ref, o_ref, sa_ref, sb_ref, *, lambda_init):
    tq = qt_ref.shape[1]
    n_chunks, _, tk = vt_ref.shape
    qt = qt_ref[...]
    row = lax.broadcasted_iota(jnp.int32, qt.shape, 0)
    zero = jnp.zeros_like(qt)
    qqt = jnp.concatenate([jnp.where(row < QK_DIM, qt, zero),
                           jnp.where(row >= QK_DIM, qt, zero)], axis=1)

    def scores(j, s_ref):
        start = pl.multiple_of(j * tk, tk)
        s = jnp.dot(k_ref[pl.ds(start, tk), :], qqt, preferred_element_type=F32)
        s_ref[...] = s
        return jnp.max(s, axis=0, keepdims=True)

    def accumulate(j, s_ref, mx, state):
        m, l, acc = state
        m_new = jnp.maximum(m, mx)
        alpha = jnp.exp2(m - m_new)
        p = jnp.exp2(s_ref[...] - m_new)
        l = alpha * l + jnp.sum(p, axis=0, keepdims=True)
        acc = alpha * acc + jnp.dot(vt_ref[j], p.astype(BF16), preferred_element_type=F32)
        return m_new, l, acc

    def body(i, carry):
        mx_a, state = carry
        j = 2 * i
        mx_b = scores(j + 1, sb_ref)
        state = accumulate(j, sa_ref, mx_a, state)
        mx_a = scores(j + 2, sa_ref)
        state = accumulate(j + 1, sb_ref, mx_b, state)
        return mx_a, state

    state = (jnp.full((1, 2 * tq), -jnp.inf, F32), jnp.zeros((1, 2 * tq), F32),
             jnp.zeros((V_DIM, 2 * tq), F32))
    mx_a, state = lax.fori_loop(0, n_chunks // 2 - 1, body, (scores(0, sa_ref), state))
    mx_b = scores(n_chunks - 1, sb_ref)
    state = accumulate(n_chunks - 2, sa_ref, mx_a, state)
    _, l, acc = accumulate(n_chunks - 1, sb_ref, mx_b, state)
    on = acc / l
    lam = (jnp.exp(jnp.sum(lam_ref[0:1, :] * lam_ref[1:2, :], axis=-1, keepdims=True))
           - jnp.exp(jnp.sum(lam_ref[2:3, :] * lam_ref[3:4, :], axis=-1, keepdims=True))
           + lambda_init)
    ot = on[:, :tq] - lam * on[:, tq:]
    ot = ot * lax.rsqrt(jnp.mean(ot * ot, axis=0, keepdims=True) + LN_EPS)
    o_ref[...] = (ot.T * g_ref[...] * (1.0 - lambda_init)).astype(BF16)


def _attention(qt, k, vt, lam_vecs, subln_g, lambda_init, tq=256):
    b, s, _ = k.shape
    n_chunks, tk = vt.shape[1], vt.shape[3]
    tq = min(tq, s)
    assert n_chunks % 2 == 0, "the chunk loop is pipelined over pairs"
    return pl.pallas_call(
        functools.partial(_attn_kernel, lambda_init=lambda_init),
        grid=(b, N_ATTN_HEADS, s // tq),
        in_specs=[pl.BlockSpec((4, QK_DIM), lambda bi, hd, i: (0, 0)),
                  pl.BlockSpec((1, V_DIM), lambda bi, hd, i: (0, 0)),
                  pl.BlockSpec((None, LANES, tq), lambda bi, hd, i: (bi, hd, i)),
                  pl.BlockSpec((None, s, LANES), lambda bi, hd, i: (bi, 0, hd)),
                  pl.BlockSpec((None, n_chunks, V_DIM, tk), lambda bi, hd, i: (bi, 0, hd, 0))],
        out_specs=pl.BlockSpec((None, tq, LANES), lambda bi, hd, i: (bi, i, hd)),
        out_shape=jax.ShapeDtypeStruct((b, s, ATTN_WIDTH), BF16),
        scratch_shapes=[pltpu.VMEM((tk, 2 * tq), F32), pltpu.VMEM((tk, 2 * tq), F32)],
        compiler_params=_params(3),
        name="attention",
    )(lam_vecs, subln_g, qt, k, vt)


def _dft_tables(n):
    idx = np.arange(n, dtype=np.int64)
    ang = (2.0 * np.pi / n) * ((idx[:, None] * idx[None, :]) % n)
    return np.cos(ang), np.sin(ang)


def _channel_table(seq):
    c, s = _dft_tables(FOURIER_GROUP_DIM)
    scale = 1.0 / math.sqrt(seq * FOURIER_GROUP_DIM)
    return jnp.asarray(np.concatenate([c, -s], axis=0) * scale, F32).astype(BF16)


def _fourier_direct_kernel(u_ref, cs_ref, ch_ref, wf_ref, o_ref):
    u = u_ref[...].astype(BF16)
    pr = jnp.dot(cs_ref[0], u, preferred_element_type=F32)
    qi = jnp.dot(cs_ref[1], u, preferred_element_type=F32)
    ch = ch_ref[...]
    for g in range(N_FOURIER_GROUPS):
        sl = slice(g * LANES, (g + 1) * LANES)
        y = jnp.concatenate([pr[:, sl], qi[:, sl]], axis=1).astype(BF16)
        f = jnp.dot(y, ch, preferred_element_type=F32)
        o_ref[:, sl] = jnp.dot(f.astype(BF16), wf_ref[g].astype(BF16),
                               preferred_element_type=F32).astype(o_ref.dtype)


def _fourier_direct(u, w_f, tt=512):
    b, s, _ = u.shape
    tt = min(tt, s)
    c, sn = _dft_tables(s)
    cs = jnp.asarray(np.stack([c, sn]), F32).astype(BF16)
    return pl.pallas_call(
        _fourier_direct_kernel,
        grid=(s // tt, b),
        in_specs=[pl.BlockSpec((None, s, FOURIER_WIDTH), lambda i, bi: (bi, 0, 0)),
                  pl.BlockSpec((2, tt, s), lambda i, bi: (0, i, 0)),
                  pl.BlockSpec((2 * FOURIER_GROUP_DIM, FOURIER_GROUP_DIM), lambda i, bi: (0, 0)),
                  pl.BlockSpec(w_f.shape, lambda i, bi: (0, 0, 0))],
        out_specs=pl.BlockSpec((None, tt, FOURIER_WIDTH), lambda i, bi: (bi, i, 0)),
        out_shape=jax.ShapeDtypeStruct((b, s, FOURIER_WIDTH), BF16),
        compiler_params=_params(2),
        name="fourier_direct",
    )(u, cs, _channel_table(s), w_f)


def _fourier_two_stage_kernel(u_ref, m1_ref, tw_ref, m2_ref, ch_ref, wf_ref, o_ref,
                              gr_ref, hr_ref, *, n1, n2):
    m1 = m1_ref[...]
    cb = tw_ref[0]
    sb = tw_ref[1]

    def stage1(s2, carry):
        cw, sw = carry
        us = u_ref[pl.ds(s2, n1, stride=n2), :].astype(BF16)
        gh = jnp.dot(m1, us, preferred_element_type=F32)
        g = gh[:n1]
        h = gh[n1:]
        row0 = pl.multiple_of(s2 * n1, n1)
        gr_ref[pl.ds(row0, n1), :] = g * cw - h * sw
        hr_ref[pl.ds(row0, n1), :] = g * sw + h * cw
        return cw * cb - sw * sb, sw * cb + cw * sb

    lax.fori_loop(0, n2, stage1, (jnp.ones_like(cb), jnp.zeros_like(sb)))

    m2 = m2_ref[...]
    ch = ch_ref[...]
    wf = wf_ref[...].astype(BF16)

    def stage2(t1, carry):
        x = jnp.concatenate([gr_ref[pl.ds(t1, n2, stride=n1), :],
                             hr_ref[pl.ds(t1, n2, stride=n1), :]], axis=0).astype(BF16)
        pq = jnp.dot(m2, x, preferred_element_type=F32)
        y = jnp.concatenate([pq[:n2], pq[n2:]], axis=1).astype(BF16)
        f = jnp.dot(y, ch, preferred_element_type=F32)
        o_ref[pl.ds(t1, n2, stride=n1), :] = jnp.dot(f.astype(BF16), wf,
                                                     preferred_element_type=F32)
        return carry

    lax.fori_loop(0, n1, stage2, 0)


def _fourier_two_stage(u, w_f, n1=DFT_STAGE1):
    b, s, _ = u.shape
    n2 = s // n1
    c1, s1 = _dft_tables(n1)
    c2, s2 = _dft_tables(n2)
    m1 = jnp.asarray(np.concatenate([c1, s1], axis=0), F32).astype(BF16)
    m2 = jnp.asarray(np.block([[c2, -s2], [s2, c2]]), F32).astype(BF16)
    ang = (2.0 * np.pi / s) * np.arange(n1, dtype=np.float64)
    tw = np.stack([np.broadcast_to(np.cos(ang)[:, None], (n1, LANES)),
                   np.broadcast_to(np.sin(ang)[:, None], (n1, LANES))])
    tw = jnp.asarray(tw, F32)
    grp = lambda bi, g: (bi, 0, g)
    return pl.pallas_call(
        functools.partial(_fourier_two_stage_kernel, n1=n1, n2=n2),
        grid=(b, N_FOURIER_GROUPS),
        in_specs=[pl.BlockSpec((None, s, LANES), grp, pipeline_mode=pl.Buffered(1)),
                  pl.BlockSpec(m1.shape, lambda bi, g: (0, 0)),
                  pl.BlockSpec(tw.shape, lambda bi, g: (0, 0, 0)),
                  pl.BlockSpec(m2.shape, lambda bi, g: (0, 0)),
                  pl.BlockSpec((2 * FOURIER_GROUP_DIM, FOURIER_GROUP_DIM), lambda bi, g: (0, 0)),
                  pl.BlockSpec((None, FOURIER_GROUP_DIM, FOURIER_GROUP_DIM), lambda bi, g: (g, 0, 0))],
        out_specs=pl.BlockSpec((None, s, LANES), grp),
        out_shape=jax.ShapeDtypeStruct((b, s, FOURIER_WIDTH), F32),
        scratch_shapes=[pltpu.VMEM((s, LANES), F32), pltpu.VMEM((s, LANES), F32)],
        compiler_params=_params(2),
        name="fourier_two_stage",
    )(u, m1, tw, m2, _channel_table(s), w_f)


def _fourier(u, w_f):
    if u.shape[1] <= DIRECT_DFT_MAX_SEQ:
        return _fourier_direct(u, w_f)
    return _fourier_two_stage(u, w_f)


def _ffn_kernel(x_ref, o_ref, f_ref, mod_ref, wo_ref, ln1g_ref, ln1b_ref,
                wg_ref, wu_ref, wd_ref, ln2g_ref, ln2b_ref, y_ref):
    mix = (jnp.dot(o_ref[...], wo_ref[:ATTN_WIDTH, :], preferred_element_type=F32)
           + jnp.dot(f_ref[...].astype(BF16), wo_ref[ATTN_WIDTH:, :], preferred_element_type=F32))
    g1 = mod_ref[2:3, :]
    sh2 = mod_ref[3:4, :]
    sc2 = mod_ref[4:5, :]
    g2 = mod_ref[5:6, :]
    x1 = _layer_norm(ALPHA * x_ref[...] + (1.0 + g1) * mix) * ln1g_ref[...] + ln1b_ref[...]
    h = (_layer_norm(x1) * (1.0 + sc2) + sh2).astype(BF16)
    gate = jnp.dot(h, wg_ref[...], preferred_element_type=F32)
    up = jnp.dot(h, wu_ref[...], preferred_element_type=F32)
    a = (_silu(gate) * up).astype(BF16)
    ffn = jnp.dot(a, wd_ref[...], preferred_element_type=F32)
    y_ref[...] = _layer_norm(ALPHA * x1 + (1.0 + g2) * ffn) * ln2g_ref[...] + ln2b_ref[...]


def _ffn(x, o, f, mod, wo, ln1g, ln1b, wg, wu, wd, ln2g, ln2b, tm=256):
    b, s, d = x.shape
    tm = min(tm, s)
    tok = lambda bi, i: (bi, i, 0)
    const = lambda bi, i: (0, 0)
    resident = lambda shape: pl.BlockSpec(shape, const, pipeline_mode=pl.Buffered(1))
    return pl.pallas_call(
        _ffn_kernel,
        grid=(b, s // tm),
        in_specs=[pl.BlockSpec((None, tm, d), tok),
                  pl.BlockSpec((None, tm, ATTN_WIDTH), tok),
                  pl.BlockSpec((None, tm, FOURIER_WIDTH), tok),
                  pl.BlockSpec((None, N_MOD, d), lambda bi, i: (bi, 0, 0)),
                  resident(wo.shape), resident((1, d)), resident((1, d)),
                  resident(wg.shape), resident(wu.shape), resident(wd.shape),
                  resident((1, d)), resident((1, d))],
        out_specs=pl.BlockSpec((None, tm, d), tok),
        out_shape=jax.ShapeDtypeStruct((b, s, d), F32),
        compiler_params=_params(2),
        name="ffn",
    )(x, o, f, mod, wo, ln1g, ln1b, wg, wu, wd, ln2g, ln2b)


def _rope_tables(seq):
    inv = ROPE_THETA ** (-jnp.arange(0, QK_DIM, 2, dtype=F32) / QK_DIM)
    ang = jnp.arange(seq, dtype=F32)[:, None] * inv[None, :]
    cos = jnp.cos(ang)
    sin = jnp.sin(ang)
    reps = LANES // (QK_DIM // 2)
    sign = jnp.tile(jnp.concatenate([-jnp.ones((QK_DIM // 2,), F32), jnp.ones((QK_DIM // 2,), F32)]),
                    LANES // QK_DIM)
    return jnp.tile(cos, (1, reps)), jnp.tile(sin, (1, reps)) * sign, cos.T, sin.T


def _layer(x, mod, l, w_qv_t, w_ku, lam_vecs, subln_g, w_fourier, w_out, ln1_g, ln1_b,
           w_gate, w_up, w_down, ln2_g, ln2_b):
    lambda_init = 0.8 - 0.6 * math.exp(-0.3 * l)
    s = x.shape[1]
    qt, k, vt, u = _inproj(x, mod, w_qv_t, w_ku, _rope_tables(s), tm=min(KV_CHUNK, s))
    o = _attention(qt, k, vt, lam_vecs, subln_g, lambda_init)
    f = _fourier(u, w_fourier)
    return _ffn(x, o, f, mod, w_out, ln1_g, ln1_b, w_gate, w_up, w_down, ln2_g, ln2_b)


def _split_w_in(w_in):
    q, k, v, u = jnp.split(w_in, [QK_WIDTH, 2 * QK_WIDTH, 2 * QK_WIDTH + ATTN_WIDTH], axis=1)
    return (jnp.concatenate([q, v], axis=1).T.astype(BF16), jnp.concatenate([k, u], axis=1).astype(BF16))


def kernel(x_prompt, x_sample, c_prompt, c_sample, w_ada, b_ada, w_in, lambda_q1, lambda_k1, lambda_q2, lambda_k2, subln_g, w_fourier, w_out, ln1_g, ln1_b, w_gate, w_up, w_down, ln2_g, ln2_b):
    n_prompt = c_prompt.shape[0]
    c_all = jnp.concatenate([c_prompt, c_sample], axis=0)
    y_prompt, y_sample = x_prompt, x_sample
    for l in range(DEPTH):
        mod = _ada(c_all, w_ada[l], b_ada[l][None, :]).reshape(c_all.shape[0], N_MOD, D_MODEL)
        lam_vecs = jnp.stack([lambda_q1[l], lambda_k1[l], lambda_q2[l], lambda_k2[l]]).astype(F32)
        weights = (*_split_w_in(w_in[l]), lam_vecs, subln_g[l][None, :], w_fourier[l],
                   w_out[l].astype(BF16), ln1_g[l][None, :], ln1_b[l][None, :],
                   w_gate[l].astype(BF16), w_up[l].astype(BF16), w_down[l].astype(BF16),
                   ln2_g[l][None, :], ln2_b[l][None, :])
        y_prompt = _layer(y_prompt, mod[:n_prompt], l, *weights)
        y_sample = _layer(y_sample, mod[n_prompt:], l, *weights)
    return (y_prompt, y_sample)
```

```python
import functools
import math

import numpy as np
import jax
import jax.numpy as jnp
from jax import lax
from jax.experimental import pallas as pl
from jax.experimental.pallas import tpu as pltpu

D_MODEL = 1024
DEPTH = 1
N_ATTN_HEADS = 4
QK_DIM = 64
V_DIM = 2 * QK_DIM
ATTN_WIDTH = N_ATTN_HEADS * V_DIM
QK_WIDTH = N_ATTN_HEADS * 2 * QK_DIM
N_FOURIER_GROUPS = 4
FOURIER_GROUP_DIM = 128
FOURIER_WIDTH = N_FOURIER_GROUPS * FOURIER_GROUP_DIM
IN_WIDTH = 2 * QK_WIDTH + ATTN_WIDTH + FOURIER_WIDTH
D_FF = -(-8 * D_MODEL // (3 * 256)) * 256
ROPE_THETA = 10000.0
LN_EPS = 1e-5
ALPHA = (2 * DEPTH) ** 0.25
N_MOD = 6

LANES = 128
VMEM_LIMIT = 56 * 1024 * 1024
DIRECT_DFT_MAX_SEQ = 2048
DFT_STAGE1 = 128
KV_CHUNK = 512
CHUNKS_PER_TRIP = 4
DENOM_ROWS = 16

F32 = jnp.float32
BF16 = jnp.bfloat16


def _params(n_grid_dims):
    return pltpu.CompilerParams(
        dimension_semantics=("arbitrary",) * n_grid_dims, vmem_limit_bytes=VMEM_LIMIT)


def _layer_norm(x):
    mu = jnp.mean(x, axis=-1, keepdims=True)
    xc = x - mu
    var = jnp.mean(xc * xc, axis=-1, keepdims=True)
    return xc * lax.rsqrt(var + LN_EPS)


def _silu(x):
    return x * (1.0 / (1.0 + jnp.exp(-x)))


def _ada_kernel(c_ref, w_ref, b_ref, o_ref):
    s = _silu(c_ref[...])
    o_ref[...] = jnp.dot(s, w_ref[...], preferred_element_type=F32,
                         precision=lax.Precision.HIGHEST) + b_ref[...]


def _ada(c, w, b, tn=1536):
    n_rows, d = c.shape
    n_out = w.shape[1]
    return pl.pallas_call(
        _ada_kernel,
        grid=(n_out // tn,),
        in_specs=[pl.BlockSpec((n_rows, d), lambda j: (0, 0)),
                  pl.BlockSpec((d, tn), lambda j: (0, j)),
                  pl.BlockSpec((1, tn), lambda j: (0, j))],
        out_specs=pl.BlockSpec((n_rows, tn), lambda j: (0, j)),
        out_shape=jax.ShapeDtypeStruct((n_rows, n_out), F32),
        compiler_params=_params(1),
        name="ada",
    )(c, w, b)


def _inproj_kernel(x_ref, mod_ref, wqv_ref, wku_ref, cos_ref, sin_ref, cost_ref, sint_ref,
                   qt_ref, k_ref, vt_ref, u_ref):
    h = (_layer_norm(x_ref[...]) * (1.0 + mod_ref[1:2, :]) + mod_ref[0:1, :]).astype(BF16)
    qvt = lax.dot_general(wqv_ref[...], h, (((1,), (1,)), ((), ())), preferred_element_type=F32)
    cost = cost_ref[...]
    sint = sint_ref[...]
    half = QK_DIM // 2
    scale = QK_DIM ** -0.5 * math.log2(math.e)
    for c in range(QK_WIDTH // QK_DIM):
        a = qvt[c * QK_DIM: c * QK_DIM + half, :]
        b = qvt[c * QK_DIM + half: (c + 1) * QK_DIM, :]
        qt_ref[c * QK_DIM: c * QK_DIM + half, :] = ((a * cost - b * sint) * scale).astype(BF16)
        qt_ref[c * QK_DIM + half: (c + 1) * QK_DIM, :] = ((b * cost + a * sint) * scale).astype(BF16)
    vt_ref[...] = qvt[QK_WIDTH:, :].astype(BF16)

    ku = jnp.dot(h, wku_ref[...], preferred_element_type=F32)
    cos = cos_ref[...]
    sin = sin_ref[...]
    lane = lax.broadcasted_iota(jnp.int32, cos.shape, 1)
    first_half = (lane % QK_DIM) < half
    for hd in range(N_ATTN_HEADS):
        xh = ku[:, hd * LANES:(hd + 1) * LANES]
        partner = jnp.where(first_half, pltpu.roll(xh, LANES - half, 1), pltpu.roll(xh, half, 1))
        k_ref[:, hd * LANES:(hd + 1) * LANES] = (xh * cos + partner * sin).astype(BF16)
    u_ref[...] = ku[:, QK_WIDTH:]


def _inproj(x, mod, w_qv_t, w_ku, rope, tm):
    b, s, d = x.shape
    cos_t, sin_t, cos_tt, sin_tt = rope
    tok = lambda bi, i: (bi, i, 0)
    const = lambda bi, i: (0, 0)
    return pl.pallas_call(
        _inproj_kernel,
        grid=(b, s // tm),
        in_specs=[pl.BlockSpec((None, tm, d), tok),
                  pl.BlockSpec((None, N_MOD, d), lambda bi, i: (bi, 0, 0)),
                  pl.BlockSpec(w_qv_t.shape, const),
                  pl.BlockSpec(w_ku.shape, const),
                  pl.BlockSpec((tm, LANES), lambda bi, i: (i, 0)),
                  pl.BlockSpec((tm, LANES), lambda bi, i: (i, 0)),
                  pl.BlockSpec((QK_DIM // 2, tm), lambda bi, i: (0, i)),
                  pl.BlockSpec((QK_DIM // 2, tm), lambda bi, i: (0, i))],
        out_specs=[pl.BlockSpec((None, QK_WIDTH, tm), lambda bi, i: (bi, 0, i)),
                   pl.BlockSpec((None, tm, QK_WIDTH), tok),
                   pl.BlockSpec((None, None, ATTN_WIDTH, tm), lambda bi, i: (bi, i, 0, 0)),
                   pl.BlockSpec((None, tm, FOURIER_WIDTH), tok)],
        out_shape=[jax.ShapeDtypeStruct((b, QK_WIDTH, s), BF16),
                   jax.ShapeDtypeStruct((b, s, QK_WIDTH), BF16),
                   jax.ShapeDtypeStruct((b, s // tm, ATTN_WIDTH, tm), BF16),
                   jax.ShapeDtypeStruct((b, s, FOURIER_WIDTH), F32)],
        compiler_params=_params(2),
        name="inproj",
    )(x, mod, w_qv_t, w_ku, cos_t, sin_t, cos_tt, sin_tt)


def _attn_kernel(lam_ref, g_ref, qt_ref, k_ref, vt_ref, o_ref, sa_ref, sb_ref, *, lambda_init):
    tq = qt_ref.shape[1]
    n_chunks, _, tk = vt_ref.shape
    qt = qt_ref[...]
    row = lax.broadcasted_iota(jnp.int32, qt.shape, 0)
    zero = jnp.zeros_like(qt)
    qqt = jnp.concatenate([jnp.where(row < QK_DIM, qt, zero),
                           jnp.where(row >= QK_DIM, qt, zero)], axis=1)

    def scores(j, s_ref):
        start = pl.multiple_of(j * tk, tk)
        s = jnp.dot(k_ref[pl.ds(start, tk), :], qqt, preferred_element_type=F32)
        s_ref[...] = s
        return jnp.max(s, axis=0, keepdims=True)

    ones = jnp.ones((DENOM_ROWS, tk), BF16)

    def accumulate(j, s_ref, mx, state):
        m, acc = state
        m_new = jnp.maximum(m, mx)
        alpha = jnp.exp2(m - m_new)
        p = jnp.exp2(s_ref[...] - m_new).astype(BF16)
        va = jnp.concatenate([vt_ref[j], ones], axis=0)
        acc = alpha * acc + jnp.dot(va, p, preferred_element_type=F32)
        return m_new, acc

    def pair(j, mx_a, state):
        mx_b = scores(j + 1, sb_ref)
        state = accumulate(j, sa_ref, mx_a, state)
        mx_a = scores(j + 2, sa_ref)
        state = accumulate(j + 1, sb_ref, mx_b, state)
        return mx_a, state

    def body(i, carry):
        for u in range(0, CHUNKS_PER_TRIP, 2):
            carry = pair(CHUNKS_PER_TRIP * i + u, *carry)
        return carry

    state = (jnp.full((1, 2 * tq), -jnp.inf, F32), jnp.zeros((V_DIM + DENOM_ROWS, 2 * tq), F32))
    n_trips = (n_chunks - 2) // CHUNKS_PER_TRIP
    carry = lax.fori_loop(0, n_trips, body, (scores(0, sa_ref), state))
    for j in range(n_trips * CHUNKS_PER_TRIP, n_chunks - 2, 2):
        carry = pair(j, *carry)
    mx_a, state = carry
    mx_b = scores(n_chunks - 1, sb_ref)
    state = accumulate(n_chunks - 2, sa_ref, mx_a, state)
    _, acc = accumulate(n_chunks - 1, sb_ref, mx_b, state)
    on = acc[:V_DIM] / acc[V_DIM:V_DIM + 1]
    lam = (jnp.exp(jnp.sum(lam_ref[0:1, :] * lam_ref[1:2, :], axis=-1, keepdims=True))
           - jnp.exp(jnp.sum(lam_ref[2:3, :] * lam_ref[3:4, :], axis=-1, keepdims=True))
           + lambda_init)
    ot = on[:, :tq] - lam * on[:, tq:]
    ot = ot * lax.rsqrt(jnp.mean(ot * ot, axis=0, keepdims=True) + LN_EPS)
    o_ref[...] = (ot.T * g_ref[...] * (1.0 - lambda_init)).astype(BF16)


def _attention(qt, k, vt, lam_vecs, subln_g, lambda_init, tq=256):
    b, s, _ = k.shape
    n_chunks, tk = vt.shape[1], vt.shape[3]
    tq = min(tq, s)
    assert n_chunks % 2 == 0, "the chunk loop is pipelined over pairs"
    return pl.pallas_call(
        functools.partial(_attn_kernel, lambda_init=lambda_init),
        grid=(b, N_ATTN_HEADS, s // tq),
        in_specs=[pl.BlockSpec((4, QK_DIM), lambda bi, hd, i: (0, 0)),
                  pl.BlockSpec((1, V_DIM), lambda bi, hd, i: (0, 0)),
                  pl.BlockSpec((None, LANES, tq), lambda bi, hd, i: (bi, hd, i)),
                  pl.BlockSpec((None, s, LANES), lambda bi, hd, i: (bi, 0, hd)),
                  pl.BlockSpec((None, n_chunks, V_DIM, tk), lambda bi, hd, i: (bi, 0, hd, 0))],
        out_specs=pl.BlockSpec((None, tq, LANES), lambda bi, hd, i: (bi, i, hd)),
        out_shape=jax.ShapeDtypeStruct((b, s, ATTN_WIDTH), BF16),
        scratch_shapes=[pltpu.VMEM((tk, 2 * tq), F32), pltpu.VMEM((tk, 2 * tq), F32)],
        compiler_params=_params(3),
        name="attention",
    )(lam_vecs, subln_g, qt, k, vt)


def _dft_tables(n):
    idx = np.arange(n, dtype=np.int64)
    ang = (2.0 * np.pi / n) * ((idx[:, None] * idx[None, :]) % n)
    return np.cos(ang), np.sin(ang)


def _channel_table(seq):
    c, s = _dft_tables(FOURIER_GROUP_DIM)
    scale = 1.0 / math.sqrt(seq * FOURIER_GROUP_DIM)
    return jnp.asarray(np.concatenate([c, -s], axis=0) * scale, F32).astype(BF16)


def _fourier_direct_kernel(u_ref, cs_ref, ch_ref, wf_ref, o_ref):
    u = u_ref[...].astype(BF16)
    pr = jnp.dot(cs_ref[0], u, preferred_element_type=F32)
    qi = jnp.dot(cs_ref[1], u, preferred_element_type=F32)
    ch = ch_ref[...]
    for g in range(N_FOURIER_GROUPS):
        sl = slice(g * LANES, (g + 1) * LANES)
        y = jnp.concatenate([pr[:, sl], qi[:, sl]], axis=1).astype(BF16)
        f = jnp.dot(y, ch, preferred_element_type=F32)
        o_ref[:, sl] = jnp.dot(f.astype(BF16), wf_ref[g].astype(BF16),
                               preferred_element_type=F32).astype(o_ref.dtype)


def _fourier_direct(u, w_f, tt=512):
    b, s, _ = u.shape
    tt = min(tt, s)
    c, sn = _dft_tables(s)
    cs = jnp.asarray(np.stack([c, sn]), F32).astype(BF16)
    return pl.pallas_call(
        _fourier_direct_kernel,
        grid=(s // tt, b),
        in_specs=[pl.BlockSpec((None, s, FOURIER_WIDTH), lambda i, bi: (bi, 0, 0)),
                  pl.BlockSpec((2, tt, s), lambda i, bi: (0, i, 0)),
                  pl.BlockSpec((2 * FOURIER_GROUP_DIM, FOURIER_GROUP_DIM), lambda i, bi: (0, 0)),
                  pl.BlockSpec(w_f.shape, lambda i, bi: (0, 0, 0))],
        out_specs=pl.BlockSpec((None, tt, FOURIER_WIDTH), lambda i, bi: (bi, i, 0)),
        out_shape=jax.ShapeDtypeStruct((b, s, FOURIER_WIDTH), BF16),
        compiler_params=_params(2),
        name="fourier_direct",
    )(u, cs, _channel_table(s), w_f)


def _fourier_two_stage_kernel(u_ref, m1_ref, tw_ref, m2_ref, ch_ref, wf_ref, o_ref,
                              er_ref, em_ref, *, n1, n2, unroll):
    m1 = m1_ref[...]
    cb = tw_ref[0]
    sb = tw_ref[1]
    ab = jnp.dot(ch_ref[...], wf_ref[...].astype(BF16), preferred_element_type=F32)
    a_c = ab[:FOURIER_GROUP_DIM]
    nb_c = ab[FOURIER_GROUP_DIM:]
    mch = jnp.concatenate([jnp.concatenate([a_c, -nb_c], axis=1),
                           jnp.concatenate([nb_c, a_c], axis=1)], axis=0).astype(BF16)

    def stage1(s2, carry):
        cw, sw = carry
        us = u_ref[pl.ds(s2, n1, stride=n2), :].astype(BF16)
        gh = jnp.dot(m1, us, preferred_element_type=F32)
        g = gh[:n1]
        h = gh[n1:]
        gt = jnp.concatenate([g * cw - h * sw, g * sw + h * cw], axis=1).astype(BF16)
        e = jnp.dot(gt, mch, preferred_element_type=F32)
        row0 = pl.multiple_of(s2 * n1, n1)
        er_ref[pl.ds(row0, n1), :] = e[:, :LANES]
        em_ref[pl.ds(row0, n1), :] = e[:, LANES:]
        return cw * cb - sw * sb, sw * cb + cw * sb

    lax.fori_loop(0, n2, stage1, (jnp.ones_like(cb), jnp.zeros_like(sb)), unroll=unroll)

    m2 = m2_ref[...]

    def stage2(t1, carry):
        x = jnp.concatenate([er_ref[pl.ds(t1, n2, stride=n1), :],
                             em_ref[pl.ds(t1, n2, stride=n1), :]], axis=0).astype(BF16)
        o_ref[pl.ds(t1, n2, stride=n1), :] = jnp.dot(m2, x, preferred_element_type=F32)
        return carry

    lax.fori_loop(0, n1, stage2, 0, unroll=unroll)


def _fourier_two_stage(u, w_f, n1=DFT_STAGE1, unroll=4):
    b, s, _ = u.shape
    n2 = s // n1
    c1, s1 = _dft_tables(n1)
    c2, s2 = _dft_tables(n2)
    m1 = jnp.asarray(np.concatenate([c1, s1], axis=0), F32).astype(BF16)
    m2 = jnp.asarray(np.concatenate([c2, -s2], axis=1), F32).astype(BF16)
    ang = (2.0 * np.pi / s) * np.arange(n1, dtype=np.float64)
    tw = np.stack([np.broadcast_to(np.cos(ang)[:, None], (n1, LANES)),
                   np.broadcast_to(np.sin(ang)[:, None], (n1, LANES))])
    tw = jnp.asarray(tw, F32)
    grp = lambda bi, g: (bi, 0, g)
    return pl.pallas_call(
        functools.partial(_fourier_two_stage_kernel, n1=n1, n2=n2, unroll=unroll),
        grid=(b, N_FOURIER_GROUPS),
        in_specs=[pl.BlockSpec((None, s, LANES), grp, pipeline_mode=pl.Buffered(1)),
                  pl.BlockSpec(m1.shape, lambda bi, g: (0, 0)),
                  pl.BlockSpec(tw.shape, lambda bi, g: (0, 0, 0)),
                  pl.BlockSpec(m2.shape, lambda bi, g: (0, 0)),
                  pl.BlockSpec((2 * FOURIER_GROUP_DIM, FOURIER_GROUP_DIM), lambda bi, g: (0, 0)),
                  pl.BlockSpec((None, FOURIER_GROUP_DIM, FOURIER_GROUP_DIM), lambda bi, g: (g, 0, 0))],
        out_specs=pl.BlockSpec((None, s, LANES), grp),
        out_shape=jax.ShapeDtypeStruct((b, s, FOURIER_WIDTH), F32),
        scratch_shapes=[pltpu.VMEM((s, LANES), F32), pltpu.VMEM((s, LANES), F32)],
        compiler_params=_params(2),
        name="fourier_two_stage",
    )(u, m1, tw, m2, _channel_table(s), w_f)


def _fourier(u, w_f):
    if u.shape[1] <= DIRECT_DFT_MAX_SEQ:
        return _fourier_direct(u, w_f)
    return _fourier_two_stage(u, w_f)


def _ffn_kernel(x_ref, o_ref, f_ref, mod_ref, wo_ref, ln1g_ref, ln1b_ref,
                wg_ref, wu_ref, wd_ref, ln2g_ref, ln2b_ref, y_ref):
    mix = (jnp.dot(o_ref[...], wo_ref[:ATTN_WIDTH, :], preferred_element_type=F32)
           + jnp.dot(f_ref[...].astype(BF16), wo_ref[ATTN_WIDTH:, :], preferred_element_type=F32))
    g1 = mod_ref[2:3, :]
    sh2 = mod_ref[3:4, :]
    sc2 = mod_ref[4:5, :]
    g2 = mod_ref[5:6, :]
    x1 = _layer_norm(ALPHA * x_ref[...] + (1.0 + g1) * mix) * ln1g_ref[...] + ln1b_ref[...]
    h = (_layer_norm(x1) * (1.0 + sc2) + sh2).astype(BF16)
    gate = jnp.dot(h, wg_ref[...], preferred_element_type=F32)
    up = jnp.dot(h, wu_ref[...], preferred_element_type=F32)
    a = (_silu(gate) * up).astype(BF16)
    ffn = jnp.dot(a, wd_ref[...], preferred_element_type=F32)
    y_ref[...] = _layer_norm(ALPHA * x1 + (1.0 + g2) * ffn) * ln2g_ref[...] + ln2b_ref[...]


def _ffn(x, o, f, mod, wo, ln1g, ln1b, wg, wu, wd, ln2g, ln2b, tm=256):
    b, s, d = x.shape
    tm = min(tm, s)
    tok = lambda bi, i: (bi, i, 0)
    const = lambda bi, i: (0, 0)
    resident = lambda shape: pl.BlockSpec(shape, const, pipeline_mode=pl.Buffered(1))
    return pl.pallas_call(
        _ffn_kernel,
        grid=(b, s // tm),
        in_specs=[pl.BlockSpec((None, tm, d), tok),
                  pl.BlockSpec((None, tm, ATTN_WIDTH), tok),
                  pl.BlockSpec((None, tm, FOURIER_WIDTH), tok),
                  pl.BlockSpec((None, N_MOD, d), lambda bi, i: (bi, 0, 0)),
                  resident(wo.shape), resident((1, d)), resident((1, d)),
                  resident(wg.shape), resident(wu.shape), resident(wd.shape),
                  resident((1, d)), resident((1, d))],
        out_specs=pl.BlockSpec((None, tm, d), tok),
        out_shape=jax.ShapeDtypeStruct((b, s, d), F32),
        compiler_params=_params(2),
        name="ffn",
    )(x, o, f, mod, wo, ln1g, ln1b, wg, wu, wd, ln2g, ln2b)


def _rope_tables(seq):
    inv = ROPE_THETA ** (-jnp.arange(0, QK_DIM, 2, dtype=F32) / QK_DIM)
    ang = jnp.arange(seq, dtype=F32)[:, None] * inv[None, :]
    cos = jnp.cos(ang)
    sin = jnp.sin(ang)
    reps = LANES // (QK_DIM // 2)
    sign = jnp.tile(jnp.concatenate([-jnp.ones((QK_DIM // 2,), F32), jnp.ones((QK_DIM // 2,), F32)]),
                    LANES // QK_DIM)
    return jnp.tile(cos, (1, reps)), jnp.tile(sin, (1, reps)) * sign, cos.T, sin.T


def _layer(x, mod, l, w_qv_t, w_ku, lam_vecs, subln_g, w_fourier, w_out, ln1_g, ln1_b,
           w_gate, w_up, w_down, ln2_g, ln2_b):
    lambda_init = 0.8 - 0.6 * math.exp(-0.3 * l)
    s = x.shape[1]
    qt, k, vt, u = _inproj(x, mod, w_qv_t, w_ku, _rope_tables(s), tm=min(KV_CHUNK, s))
    o = _attention(qt, k, vt, lam_vecs, subln_g, lambda_init)
    f = _fourier(u, w_fourier)
    return _ffn(x, o, f, mod, w_out, ln1_g, ln1_b, w_gate, w_up, w_down, ln2_g, ln2_b)


def _split_w_in(w_in):
    q, k, v, u = jnp.split(w_in, [QK_WIDTH, 2 * QK_WIDTH, 2 * QK_WIDTH + ATTN_WIDTH], axis=1)
    return (jnp.concatenate([q, v], axis=1).T.astype(BF16), jnp.concatenate([k, u], axis=1).astype(BF16))


def kernel(x_prompt, x_sample, c_prompt, c_sample, w_ada, b_ada, w_in, lambda_q1, lambda_k1, lambda_q2, lambda_k2, subln_g, w_fourier, w_out, ln1_g, ln1_b, w_gate, w_up, w_down, ln2_g, ln2_b):
    n_prompt = c_prompt.shape[0]
    c_all = jnp.concatenate([c_prompt, c_sample], axis=0)
    y_prompt, y_sample = x_prompt, x_sample
    for l in range(DEPTH):
        mod = _ada(c_all, w_ada[l], b_ada[l][None, :]).reshape(c_all.shape[0], N_MOD, D_MODEL)
        lam_vecs = jnp.stack([lambda_q1[l], lambda_k1[l], lambda_q2[l], lambda_k2[l]]).astype(F32)
        weights = (*_split_w_in(w_in[l]), lam_vecs, subln_g[l][None, :], w_fourier[l],
                   w_out[l].astype(BF16), ln1_g[l][None, :], ln1_b[l][None, :],
                   w_gate[l].astype(BF16), w_up[l].astype(BF16), w_down[l].astype(BF16),
                   ln2_g[l][None, :], ln2_b[l][None, :])
        y_prompt = _layer(y_prompt, mod[:n_prompt], l, *weights)
        y_sample = _layer(y_sample, mod[n_prompt:], l, *weights)
    return (y_prompt, y_sample)
```

```python
import functools
import math

import numpy as np
import jax
import jax.numpy as jnp
from jax import lax
from jax.experimental import pallas as pl
from jax.experimental.pallas import tpu as pltpu

D_MODEL = 1024
DEPTH = 1
N_ATTN_HEADS = 4
QK_DIM = 64
V_DIM = 2 * QK_DIM
ATTN_WIDTH = N_ATTN_HEADS * V_DIM
QK_WIDTH = N_ATTN_HEADS * 2 * QK_DIM
N_FOURIER_GROUPS = 4
FOURIER_GROUP_DIM = 128
FOURIER_WIDTH = N_FOURIER_GROUPS * FOURIER_GROUP_DIM
IN_WIDTH = 2 * QK_WIDTH + ATTN_WIDTH + FOURIER_WIDTH
D_FF = -(-8 * D_MODEL // (3 * 256)) * 256
ROPE_THETA = 10000.0
LN_EPS = 1e-5
ALPHA = (2 * DEPTH) ** 0.25
N_MOD = 6

LANES = 128
VMEM_LIMIT = 56 * 1024 * 1024
DIRECT_DFT_MAX_SEQ = 2048
DFT_STAGE1 = 128
KV_CHUNK = 512
CHUNKS_PER_TRIP = 10
DENOM_ROWS = 16

F32 = jnp.float32
BF16 = jnp.bfloat16


def _params(n_grid_dims):
    return pltpu.CompilerParams(
        dimension_semantics=("arbitrary",) * n_grid_dims, vmem_limit_bytes=VMEM_LIMIT)


def _layer_norm(x):
    mu = jnp.mean(x, axis=-1, keepdims=True)
    xc = x - mu
    var = jnp.mean(xc * xc, axis=-1, keepdims=True)
    return xc * lax.rsqrt(var + LN_EPS)


def _silu(x):
    return x * (1.0 / (1.0 + jnp.exp(-x)))


def _ada_kernel(c_ref, w_ref, b_ref, o_ref):
    s = _silu(c_ref[...])
    o_ref[...] = jnp.dot(s, w_ref[...], preferred_element_type=F32,
                         precision=lax.Precision.HIGHEST) + b_ref[...]


def _ada(c, w, b, tn=1536):
    n_rows, d = c.shape
    n_out = w.shape[1]
    return pl.pallas_call(
        _ada_kernel,
        grid=(n_out // tn,),
        in_specs=[pl.BlockSpec((n_rows, d), lambda j: (0, 0)),
                  pl.BlockSpec((d, tn), lambda j: (0, j)),
                  pl.BlockSpec((1, tn), lambda j: (0, j))],
        out_specs=pl.BlockSpec((n_rows, tn), lambda j: (0, j)),
        out_shape=jax.ShapeDtypeStruct((n_rows, n_out), F32),
        compiler_params=_params(1),
        name="ada",
    )(c, w, b)


def _inproj_kernel(x_ref, mod_ref, wqv_ref, wku_ref, cos_ref, sin_ref, cost_ref, sint_ref,
                   qt_ref, k_ref, vt_ref, u_ref):
    h = (_layer_norm(x_ref[...]) * (1.0 + mod_ref[1:2, :]) + mod_ref[0:1, :]).astype(BF16)
    qvt = lax.dot_general(wqv_ref[...], h, (((1,), (1,)), ((), ())), preferred_element_type=F32)
    cost = cost_ref[...]
    sint = sint_ref[...]
    half = QK_DIM // 2
    scale = QK_DIM ** -0.5 * math.log2(math.e)
    for c in range(QK_WIDTH // QK_DIM):
        a = qvt[c * QK_DIM: c * QK_DIM + half, :]
        b = qvt[c * QK_DIM + half: (c + 1) * QK_DIM, :]
        qt_ref[c * QK_DIM: c * QK_DIM + half, :] = ((a * cost - b * sint) * scale).astype(BF16)
        qt_ref[c * QK_DIM + half: (c + 1) * QK_DIM, :] = ((b * cost + a * sint) * scale).astype(BF16)
    vt_ref[...] = qvt[QK_WIDTH:, :].astype(BF16)

    ku = jnp.dot(h, wku_ref[...], preferred_element_type=F32)
    cos = cos_ref[...]
    sin = sin_ref[...]
    lane = lax.broadcasted_iota(jnp.int32, cos.shape, 1)
    first_half = (lane % QK_DIM) < half
    for hd in range(N_ATTN_HEADS):
        xh = ku[:, hd * LANES:(hd + 1) * LANES]
        partner = jnp.where(first_half, pltpu.roll(xh, LANES - half, 1), pltpu.roll(xh, half, 1))
        k_ref[:, hd * LANES:(hd + 1) * LANES] = (xh * cos + partner * sin).astype(BF16)
    u_ref[...] = ku[:, QK_WIDTH:]


def _inproj(x, mod, w_qv_t, w_ku, rope, tm):
    b, s, d = x.shape
    cos_t, sin_t, cos_tt, sin_tt = rope
    tok = lambda bi, i: (bi, i, 0)
    const = lambda bi, i: (0, 0)
    return pl.pallas_call(
        _inproj_kernel,
        grid=(b, s // tm),
        in_specs=[pl.BlockSpec((None, tm, d), tok),
                  pl.BlockSpec((None, N_MOD, d), lambda bi, i: (bi, 0, 0)),
                  pl.BlockSpec(w_qv_t.shape, const),
                  pl.BlockSpec(w_ku.shape, const),
                  pl.BlockSpec((tm, LANES), lambda bi, i: (i, 0)),
                  pl.BlockSpec((tm, LANES), lambda bi, i: (i, 0)),
                  pl.BlockSpec((QK_DIM // 2, tm), lambda bi, i: (0, i)),
                  pl.BlockSpec((QK_DIM // 2, tm), lambda bi, i: (0, i))],
        out_specs=[pl.BlockSpec((None, QK_WIDTH, tm), lambda bi, i: (bi, 0, i)),
                   pl.BlockSpec((None, tm, QK_WIDTH), tok),
                   pl.BlockSpec((None, None, ATTN_WIDTH, tm), lambda bi, i: (bi, i, 0, 0)),
                   pl.BlockSpec((None, tm, FOURIER_WIDTH), tok)],
        out_shape=[jax.ShapeDtypeStruct((b, QK_WIDTH, s), BF16),
                   jax.ShapeDtypeStruct((b, s, QK_WIDTH), BF16),
                   jax.ShapeDtypeStruct((b, s // tm, ATTN_WIDTH, tm), BF16),
                   jax.ShapeDtypeStruct((b, s, FOURIER_WIDTH), F32)],
        compiler_params=_params(2),
        name="inproj",
    )(x, mod, w_qv_t, w_ku, cos_t, sin_t, cos_tt, sin_tt)


def _attn_kernel(lam_ref, g_ref, qt_ref, k_ref, vt_ref, o_ref, sa_ref, sb_ref, *, lambda_init):
    tq = qt_ref.shape[1]
    n_chunks, _, tk = vt_ref.shape
    row = lax.broadcasted_iota(jnp.int32, (LANES, tq), 0)
    ones = jnp.ones((DENOM_ROWS, tk), BF16)
    lam = (jnp.exp(jnp.sum(lam_ref[0:1, :] * lam_ref[1:2, :], axis=-1, keepdims=True))
           - jnp.exp(jnp.sum(lam_ref[2:3, :] * lam_ref[3:4, :], axis=-1, keepdims=True))
           + lambda_init)

    def queries(hd):
        qt = qt_ref[hd * LANES:(hd + 1) * LANES, :]
        zero = jnp.zeros_like(qt)
        return jnp.concatenate([jnp.where(row < QK_DIM, qt, zero),
                                jnp.where(row >= QK_DIM, qt, zero)], axis=1)

    def scores(hd, qqt, j, s_ref):
        start = pl.multiple_of(j * tk, tk)
        kc = k_ref[pl.ds(start, tk), hd * LANES:(hd + 1) * LANES]
        s = jnp.dot(kc, qqt, preferred_element_type=F32)
        s_ref[...] = s
        return jnp.max(s, axis=0, keepdims=True)

    def accumulate(hd, j, s_ref, mx, state):
        m, acc = state
        m_new = jnp.maximum(m, mx)
        alpha = jnp.exp2(m - m_new)
        p = jnp.exp2(s_ref[...] - m_new).astype(BF16)
        va = jnp.concatenate([vt_ref[j, hd * V_DIM:(hd + 1) * V_DIM, :], ones], axis=0)
        acc = alpha * acc + jnp.dot(va, p, preferred_element_type=F32)
        return m_new, acc

    def finalize(hd, acc):
        on = acc[:V_DIM] / acc[V_DIM:V_DIM + 1]
        ot = on[:, :tq] - lam * on[:, tq:]
        ot = ot * lax.rsqrt(jnp.mean(ot * ot, axis=0, keepdims=True) + LN_EPS)
        o_ref[:, hd * V_DIM:(hd + 1) * V_DIM] = (ot.T * g_ref[...] * (1.0 - lambda_init)).astype(BF16)

    n_trips = (n_chunks - 2) // CHUNKS_PER_TRIP
    qqt = queries(0)
    mx_a = scores(0, qqt, 0, sa_ref)
    for hd in range(N_ATTN_HEADS):
        def pair(j, mx_a, state, hd=hd, qqt=qqt):
            mx_b = scores(hd, qqt, j + 1, sb_ref)
            state = accumulate(hd, j, sa_ref, mx_a, state)
            mx_a = scores(hd, qqt, j + 2, sa_ref)
            state = accumulate(hd, j + 1, sb_ref, mx_b, state)
            return mx_a, state

        def body(i, carry, pair=pair):
            for u in range(0, CHUNKS_PER_TRIP, 2):
                carry = pair(CHUNKS_PER_TRIP * i + u, *carry)
            return carry

        state = (jnp.full((1, 2 * tq), -jnp.inf, F32), jnp.zeros((V_DIM + DENOM_ROWS, 2 * tq), F32))
        carry = lax.fori_loop(0, n_trips, body, (mx_a, state))
        for j in range(n_trips * CHUNKS_PER_TRIP, n_chunks - 2, 2):
            carry = pair(j, *carry)
        mx_a, state = carry
        mx_b = scores(hd, qqt, n_chunks - 1, sb_ref)
        state = accumulate(hd, n_chunks - 2, sa_ref, mx_a, state)
        if hd + 1 < N_ATTN_HEADS:
            qqt = queries(hd + 1)
            mx_a = scores(hd + 1, qqt, 0, sa_ref)
        _, acc = accumulate(hd, n_chunks - 1, sb_ref, mx_b, state)
        finalize(hd, acc)


def _attention(qt, k, vt, lam_vecs, subln_g, lambda_init, tq=256):
    b, s, _ = k.shape
    n_chunks, tk = vt.shape[1], vt.shape[3]
    tq = min(tq, s)
    assert n_chunks % 2 == 0, "the chunk loop is pipelined over pairs"
    once = pl.Buffered(1)
    return pl.pallas_call(
        functools.partial(_attn_kernel, lambda_init=lambda_init),
        grid=(b, s // tq),
        in_specs=[pl.BlockSpec((4, QK_DIM), lambda bi, i: (0, 0)),
                  pl.BlockSpec((1, V_DIM), lambda bi, i: (0, 0)),
                  pl.BlockSpec((None, QK_WIDTH, tq), lambda bi, i: (bi, 0, i)),
                  pl.BlockSpec((None, s, QK_WIDTH), lambda bi, i: (bi, 0, 0), pipeline_mode=once),
                  pl.BlockSpec((None, n_chunks, ATTN_WIDTH, tk), lambda bi, i: (bi, 0, 0, 0),
                               pipeline_mode=once)],
        out_specs=pl.BlockSpec((None, tq, ATTN_WIDTH), lambda bi, i: (bi, i, 0)),
        out_shape=jax.ShapeDtypeStruct((b, s, ATTN_WIDTH), BF16),
        scratch_shapes=[pltpu.VMEM((tk, 2 * tq), F32), pltpu.VMEM((tk, 2 * tq), F32)],
        compiler_params=_params(2),
        name="attention",
    )(lam_vecs, subln_g, qt, k, vt)


def _dft_tables(n):
    idx = np.arange(n, dtype=np.int64)
    ang = (2.0 * np.pi / n) * ((idx[:, None] * idx[None, :]) % n)
    return np.cos(ang), np.sin(ang)


def _channel_table(seq):
    c, s = _dft_tables(FOURIER_GROUP_DIM)
    scale = 1.0 / math.sqrt(seq * FOURIER_GROUP_DIM)
    return jnp.asarray(np.concatenate([c, -s], axis=0) * scale, F32).astype(BF16)


def _fourier_direct_kernel(u_ref, cs_ref, ch_ref, wf_ref, o_ref):
    u = u_ref[...].astype(BF16)
    pr = jnp.dot(cs_ref[0], u, preferred_element_type=F32)
    qi = jnp.dot(cs_ref[1], u, preferred_element_type=F32)
    ch = ch_ref[...]
    for g in range(N_FOURIER_GROUPS):
        sl = slice(g * LANES, (g + 1) * LANES)
        y = jnp.concatenate([pr[:, sl], qi[:, sl]], axis=1).astype(BF16)
        f = jnp.dot(y, ch, preferred_element_type=F32)
        o_ref[:, sl] = jnp.dot(f.astype(BF16), wf_ref[g].astype(BF16),
                               preferred_element_type=F32).astype(o_ref.dtype)


def _fourier_direct(u, w_f, tt=512):
    b, s, _ = u.shape
    tt = min(tt, s)
    c, sn = _dft_tables(s)
    cs = jnp.asarray(np.stack([c, sn]), F32).astype(BF16)
    return pl.pallas_call(
        _fourier_direct_kernel,
        grid=(s // tt, b),
        in_specs=[pl.BlockSpec((None, s, FOURIER_WIDTH), lambda i, bi: (bi, 0, 0)),
                  pl.BlockSpec((2, tt, s), lambda i, bi: (0, i, 0)),
                  pl.BlockSpec((2 * FOURIER_GROUP_DIM, FOURIER_GROUP_DIM), lambda i, bi: (0, 0)),
                  pl.BlockSpec(w_f.shape, lambda i, bi: (0, 0, 0))],
        out_specs=pl.BlockSpec((None, tt, FOURIER_WIDTH), lambda i, bi: (bi, i, 0)),
        out_shape=jax.ShapeDtypeStruct((b, s, FOURIER_WIDTH), BF16),
        compiler_params=_params(2),
        name="fourier_direct",
    )(u, cs, _channel_table(s), w_f)


def _fourier_two_stage_kernel(u_ref, m1_ref, tw_ref, m2_ref, ch_ref, wf_ref, o_ref,
                              er_ref, em_ref, *, n1, n2, unroll):
    m1 = m1_ref[...]
    cb = tw_ref[0]
    sb = tw_ref[1]
    ab = jnp.dot(ch_ref[...], wf_ref[...].astype(BF16), preferred_element_type=F32)
    a_c = ab[:FOURIER_GROUP_DIM]
    nb_c = ab[FOURIER_GROUP_DIM:]
    mch = jnp.concatenate([jnp.concatenate([a_c, -nb_c], axis=1),
                           jnp.concatenate([nb_c, a_c], axis=1)], axis=0).astype(BF16)

    def stage1(s2, carry):
        cw, sw = carry
        us = u_ref[pl.ds(s2, n1, stride=n2), :].astype(BF16)
        gh = jnp.dot(m1, us, preferred_element_type=F32)
        g = gh[:n1]
        h = gh[n1:]
        gt = jnp.concatenate([g * cw - h * sw, g * sw + h * cw], axis=1).astype(BF16)
        e = jnp.dot(gt, mch, preferred_element_type=F32)
        row0 = pl.multiple_of(s2 * n1, n1)
        er_ref[pl.ds(row0, n1), :] = e[:, :LANES]
        em_ref[pl.ds(row0, n1), :] = e[:, LANES:]
        return cw * cb - sw * sb, sw * cb + cw * sb

    lax.fori_loop(0, n2, stage1, (jnp.ones_like(cb), jnp.zeros_like(sb)), unroll=unroll)

    m2 = m2_ref[...]

    def stage2(t1, carry):
        x = jnp.concatenate([er_ref[pl.ds(t1, n2, stride=n1), :],
                             em_ref[pl.ds(t1, n2, stride=n1), :]], axis=0).astype(BF16)
        o_ref[pl.ds(t1, n2, stride=n1), :] = jnp.dot(m2, x, preferred_element_type=F32)
        return carry

    lax.fori_loop(0, n1, stage2, 0, unroll=unroll)


def _fourier_two_stage(u, w_f, n1=DFT_STAGE1, unroll=4):
    b, s, _ = u.shape
    n2 = s // n1
    c1, s1 = _dft_tables(n1)
    c2, s2 = _dft_tables(n2)
    m1 = jnp.asarray(np.concatenate([c1, s1], axis=0), F32).astype(BF16)
    m2 = jnp.asarray(np.concatenate([c2, -s2], axis=1), F32).astype(BF16)
    ang = (2.0 * np.pi / s) * np.arange(n1, dtype=np.float64)
    tw = np.stack([np.broadcast_to(np.cos(ang)[:, None], (n1, LANES)),
                   np.broadcast_to(np.sin(ang)[:, None], (n1, LANES))])
    tw = jnp.asarray(tw, F32)
    grp = lambda bi, g: (bi, 0, g)
    return pl.pallas_call(
        functools.partial(_fourier_two_stage_kernel, n1=n1, n2=n2, unroll=unroll),
        grid=(b, N_FOURIER_GROUPS),
        in_specs=[pl.BlockSpec((None, s, LANES), grp, pipeline_mode=pl.Buffered(1)),
                  pl.BlockSpec(m1.shape, lambda bi, g: (0, 0)),
                  pl.BlockSpec(tw.shape, lambda bi, g: (0, 0, 0)),
                  pl.BlockSpec(m2.shape, lambda bi, g: (0, 0)),
                  pl.BlockSpec((2 * FOURIER_GROUP_DIM, FOURIER_GROUP_DIM), lambda bi, g: (0, 0)),
                  pl.BlockSpec((None, FOURIER_GROUP_DIM, FOURIER_GROUP_DIM), lambda bi, g: (g, 0, 0))],
        out_specs=pl.BlockSpec((None, s, LANES), grp),
        out_shape=jax.ShapeDtypeStruct((b, s, FOURIER_WIDTH), F32),
        scratch_shapes=[pltpu.VMEM((s, LANES), F32), pltpu.VMEM((s, LANES), F32)],
        compiler_params=_params(2),
        name="fourier_two_stage",
    )(u, m1, tw, m2, _channel_table(s), w_f)


def _fourier(u, w_f):
    if u.shape[1] <= DIRECT_DFT_MAX_SEQ:
        return _fourier_direct(u, w_f)
    return _fourier_two_stage(u, w_f)


def _ffn_kernel(x_ref, o_ref, f_ref, mod_ref, wo_ref, ln1g_ref, ln1b_ref,
                wg_ref, wu_ref, wd_ref, ln2g_ref, ln2b_ref, y_ref, *, n_sub):
    g1 = mod_ref[2:3, :]
    sh2 = mod_ref[3:4, :]
    sc2 = mod_ref[4:5, :]
    g2 = mod_ref[5:6, :]
    rows = x_ref.shape[0] // n_sub
    for r in range(n_sub):
        sl = slice(r * rows, (r + 1) * rows)
        mix = (jnp.dot(o_ref[sl, :], wo_ref[:ATTN_WIDTH, :], preferred_element_type=F32)
               + jnp.dot(f_ref[sl, :].astype(BF16), wo_ref[ATTN_WIDTH:, :], preferred_element_type=F32))
        x1 = _layer_norm(ALPHA * x_ref[sl, :] + (1.0 + g1) * mix) * ln1g_ref[...] + ln1b_ref[...]
        h = (_layer_norm(x1) * (1.0 + sc2) + sh2).astype(BF16)
        gate = jnp.dot(h, wg_ref[...], preferred_element_type=F32)
        up = jnp.dot(h, wu_ref[...], preferred_element_type=F32)
        a = (_silu(gate) * up).astype(BF16)
        ffn = jnp.dot(a, wd_ref[...], preferred_element_type=F32)
        y_ref[sl, :] = _layer_norm(ALPHA * x1 + (1.0 + g2) * ffn) * ln2g_ref[...] + ln2b_ref[...]


def _ffn(x, o, f, mod, wo, ln1g, ln1b, wg, wu, wd, ln2g, ln2b, tm=512, n_sub=2):
    b, s, d = x.shape
    tm = min(tm, s)
    tok = lambda bi, i: (bi, i, 0)
    const = lambda bi, i: (0, 0)
    resident = lambda shape: pl.BlockSpec(shape, const, pipeline_mode=pl.Buffered(1))
    return pl.pallas_call(
        functools.partial(_ffn_kernel, n_sub=n_sub),
        grid=(b, s // tm),
        in_specs=[pl.BlockSpec((None, tm, d), tok),
                  pl.BlockSpec((None, tm, ATTN_WIDTH), tok),
                  pl.BlockSpec((None, tm, FOURIER_WIDTH), tok),
                  pl.BlockSpec((None, N_MOD, d), lambda bi, i: (bi, 0, 0)),
                  resident(wo.shape), resident((1, d)), resident((1, d)),
                  resident(wg.shape), resident(wu.shape), resident(wd.shape),
                  resident((1, d)), resident((1, d))],
        out_specs=pl.BlockSpec((None, tm, d), tok),
        out_shape=jax.ShapeDtypeStruct((b, s, d), F32),
        compiler_params=_params(2),
        name="ffn",
    )(x, o, f, mod, wo, ln1g, ln1b, wg, wu, wd, ln2g, ln2b)


def _rope_tables(seq):
    inv = ROPE_THETA ** (-jnp.arange(0, QK_DIM, 2, dtype=F32) / QK_DIM)
    ang = jnp.arange(seq, dtype=F32)[:, None] * inv[None, :]
    cos = jnp.cos(ang)
    sin = jnp.sin(ang)
    reps = LANES // (QK_DIM // 2)
    sign = jnp.tile(jnp.concatenate([-jnp.ones((QK_DIM // 2,), F32), jnp.ones((QK_DIM // 2,), F32)]),
                    LANES // QK_DIM)
    return jnp.tile(cos, (1, reps)), jnp.tile(sin, (1, reps)) * sign, cos.T, sin.T


def _layer(x, mod, l, w_qv_t, w_ku, lam_vecs, subln_g, w_fourier, w_out, ln1_g, ln1_b,
           w_gate, w_up, w_down, ln2_g, ln2_b):
    lambda_init = 0.8 - 0.6 * math.exp(-0.3 * l)
    s = x.shape[1]
    qt, k, vt, u = _inproj(x, mod, w_qv_t, w_ku, _rope_tables(s), tm=min(KV_CHUNK, s))
    o = _attention(qt, k, vt, lam_vecs, subln_g, lambda_init)
    f = _fourier(u, w_fourier)
    return _ffn(x, o, f, mod, w_out, ln1_g, ln1_b, w_gate, w_up, w_down, ln2_g, ln2_b)


def _split_w_in(w_in):
    q, k, v, u = jnp.split(w_in, [QK_WIDTH, 2 * QK_WIDTH, 2 * QK_WIDTH + ATTN_WIDTH], axis=1)
    return (jnp.concatenate([q, v], axis=1).T.astype(BF16), jnp.concatenate([k, u], axis=1).astype(BF16))


def kernel(x_prompt, x_sample, c_prompt, c_sample, w_ada, b_ada, w_in, lambda_q1, lambda_k1, lambda_q2, lambda_k2, subln_g, w_fourier, w_out, ln1_g, ln1_b, w_gate, w_up, w_down, ln2_g, ln2_b):
    n_prompt = c_prompt.shape[0]
    c_all = jnp.concatenate([c_prompt, c_sample], axis=0)
    y_prompt, y_sample = x_prompt, x_sample
    for l in range(DEPTH):
        mod = _ada(c_all, w_ada[l], b_ada[l][None, :]).reshape(c_all.shape[0], N_MOD, D_MODEL)
        lam_vecs = jnp.stack([lambda_q1[l], lambda_k1[l], lambda_q2[l], lambda_k2[l]]).astype(F32)
        weights = (*_split_w_in(w_in[l]), lam_vecs, subln_g[l][None, :], w_fourier[l],
                   w_out[l].astype(BF16), ln1_g[l][None, :], ln1_b[l][None, :],
                   w_gate[l].astype(BF16), w_up[l].astype(BF16), w_down[l].astype(BF16),
                   ln2_g[l][None, :], ln2_b[l][None, :])
        y_prompt = _layer(y_prompt, mod[:n_prompt], l, *weights)
        y_sample = _layer(y_sample, mod[n_prompt:], l, *weights)
    return (y_prompt, y_sample)
```

```python
import functools
import math

import numpy as np
import jax
import jax.numpy as jnp
from jax import lax
from jax.experimental import pallas as pl
from jax.experimental.pallas import tpu as pltpu

D_MODEL = 1024
DEPTH = 1
N_ATTN_HEADS = 4
QK_DIM = 64
V_DIM = 2 * QK_DIM
ATTN_WIDTH = N_ATTN_HEADS * V_DIM
QK_WIDTH = N_ATTN_HEADS * 2 * QK_DIM
N_FOURIER_GROUPS = 4
FOURIER_GROUP_DIM = 128
FOURIER_WIDTH = N_FOURIER_GROUPS * FOURIER_GROUP_DIM
IN_WIDTH = 2 * QK_WIDTH + ATTN_WIDTH + FOURIER_WIDTH
D_FF = -(-8 * D_MODEL // (3 * 256)) * 256
ROPE_THETA = 10000.0
LN_EPS = 1e-5
ALPHA = (2 * DEPTH) ** 0.25
N_MOD = 6

LANES = 128
VMEM_LIMIT = 56 * 1024 * 1024
DIRECT_DFT_MAX_SEQ = 2048
DFT_STAGE1 = 128
KV_CHUNK = 512
CHUNKS_PER_TRIP = 10
DENOM_ROWS = 16

F32 = jnp.float32
BF16 = jnp.bfloat16


def _params(n_grid_dims):
    return pltpu.CompilerParams(
        dimension_semantics=("arbitrary",) * n_grid_dims, vmem_limit_bytes=VMEM_LIMIT)


def _layer_norm(x):
    mu = jnp.mean(x, axis=-1, keepdims=True)
    xc = x - mu
    var = jnp.mean(xc * xc, axis=-1, keepdims=True)
    return xc * lax.rsqrt(var + LN_EPS)


def _silu(x):
    return x * (1.0 / (1.0 + jnp.exp(-x)))


def _ada_kernel(c_ref, w_ref, b_ref, o_ref):
    s = _silu(c_ref[...])
    o_ref[...] = jnp.dot(s, w_ref[...], preferred_element_type=F32,
                         precision=lax.Precision.HIGHEST) + b_ref[...]


def _ada(c, w, b, tn=1536):
    n_rows, d = c.shape
    n_out = w.shape[1]
    return pl.pallas_call(
        _ada_kernel,
        grid=(n_out // tn,),
        in_specs=[pl.BlockSpec((n_rows, d), lambda j: (0, 0)),
                  pl.BlockSpec((d, tn), lambda j: (0, j)),
                  pl.BlockSpec((1, tn), lambda j: (0, j))],
        out_specs=pl.BlockSpec((n_rows, tn), lambda j: (0, j)),
        out_shape=jax.ShapeDtypeStruct((n_rows, n_out), F32),
        compiler_params=_params(1),
        name="ada",
    )(c, w, b)


def _inproj_kernel(x_ref, mod_ref, wqv_ref, wku_ref, cos_ref, sin_ref, cost_ref, sint_ref,
                   qt_ref, k_ref, vt_ref, u_ref):
    h = (_layer_norm(x_ref[...]) * (1.0 + mod_ref[1:2, :]) + mod_ref[0:1, :]).astype(BF16)
    qvt = lax.dot_general(wqv_ref[...], h, (((1,), (1,)), ((), ())), preferred_element_type=F32)
    cost = cost_ref[...]
    sint = sint_ref[...]
    half = QK_DIM // 2
    scale = QK_DIM ** -0.5 * math.log2(math.e)
    for c in range(QK_WIDTH // QK_DIM):
        a = qvt[c * QK_DIM: c * QK_DIM + half, :]
        b = qvt[c * QK_DIM + half: (c + 1) * QK_DIM, :]
        qt_ref[c * QK_DIM: c * QK_DIM + half, :] = ((a * cost - b * sint) * scale).astype(BF16)
        qt_ref[c * QK_DIM + half: (c + 1) * QK_DIM, :] = ((b * cost + a * sint) * scale).astype(BF16)
    vt_ref[...] = qvt[QK_WIDTH:, :].astype(BF16)

    ku = jnp.dot(h, wku_ref[...], preferred_element_type=F32)
    cos = cos_ref[...]
    sin = sin_ref[...]
    lane = lax.broadcasted_iota(jnp.int32, cos.shape, 1)
    first_half = (lane % QK_DIM) < half
    for hd in range(N_ATTN_HEADS):
        xh = ku[:, hd * LANES:(hd + 1) * LANES]
        partner = jnp.where(first_half, pltpu.roll(xh, LANES - half, 1), pltpu.roll(xh, half, 1))
        k_ref[:, hd * LANES:(hd + 1) * LANES] = (xh * cos + partner * sin).astype(BF16)
    u_ref[...] = ku[:, QK_WIDTH:]


def _inproj(x, mod, w_qv_t, w_ku, rope, tm):
    b, s, d = x.shape
    cos_t, sin_t, cos_tt, sin_tt = rope
    tok = lambda bi, i: (bi, i, 0)
    const = lambda bi, i: (0, 0)
    return pl.pallas_call(
        _inproj_kernel,
        grid=(b, s // tm),
        in_specs=[pl.BlockSpec((None, tm, d), tok),
                  pl.BlockSpec((None, N_MOD, d), lambda bi, i: (bi, 0, 0)),
                  pl.BlockSpec(w_qv_t.shape, const),
                  pl.BlockSpec(w_ku.shape, const),
                  pl.BlockSpec((tm, LANES), lambda bi, i: (i, 0)),
                  pl.BlockSpec((tm, LANES), lambda bi, i: (i, 0)),
                  pl.BlockSpec((QK_DIM // 2, tm), lambda bi, i: (0, i)),
                  pl.BlockSpec((QK_DIM // 2, tm), lambda bi, i: (0, i))],
        out_specs=[pl.BlockSpec((None, QK_WIDTH, tm), lambda bi, i: (bi, 0, i)),
                   pl.BlockSpec((None, tm, QK_WIDTH), tok),
                   pl.BlockSpec((None, None, ATTN_WIDTH, tm), lambda bi, i: (bi, i, 0, 0)),
                   pl.BlockSpec((None, tm, FOURIER_WIDTH), tok)],
        out_shape=[jax.ShapeDtypeStruct((b, QK_WIDTH, s), BF16),
                   jax.ShapeDtypeStruct((b, s, QK_WIDTH), BF16),
                   jax.ShapeDtypeStruct((b, s // tm, ATTN_WIDTH, tm), BF16),
                   jax.ShapeDtypeStruct((b, s, FOURIER_WIDTH), F32)],
        compiler_params=_params(2),
        name="inproj",
    )(x, mod, w_qv_t, w_ku, cos_t, sin_t, cos_tt, sin_tt)


def _attn_kernel(lam_ref, g_ref, qt_ref, k_ref, vt_ref, o_ref, sa_ref, sb_ref, *, lambda_init):
    tq = qt_ref.shape[1]
    n_chunks, _, tk = vt_ref.shape
    row = lax.broadcasted_iota(jnp.int32, (LANES, tq), 0)
    ones = jnp.ones((DENOM_ROWS, tk), BF16)
    lam = (jnp.exp(jnp.sum(lam_ref[0:1, :] * lam_ref[1:2, :], axis=-1, keepdims=True))
           - jnp.exp(jnp.sum(lam_ref[2:3, :] * lam_ref[3:4, :], axis=-1, keepdims=True))
           + lambda_init)

    def queries(hd):
        qt = qt_ref[hd * LANES:(hd + 1) * LANES, :]
        zero = jnp.zeros_like(qt)
        return jnp.concatenate([jnp.where(row < QK_DIM, qt, zero),
                                jnp.where(row >= QK_DIM, qt, zero)], axis=1)

    def scores(hd, qqt, j, s_ref):
        start = pl.multiple_of(j * tk, tk)
        kc = k_ref[pl.ds(start, tk), hd * LANES:(hd + 1) * LANES]
        s = jnp.dot(kc, qqt, preferred_element_type=F32)
        s_ref[...] = s
        return jnp.max(s, axis=0, keepdims=True)

    def accumulate(hd, j, s_ref, mx, state):
        m, acc = state
        m_new = jnp.maximum(m, mx)
        alpha = jnp.exp2(m - m_new)
        p = jnp.exp2(s_ref[...] - m_new).astype(BF16)
        va = jnp.concatenate([vt_ref[j, hd * V_DIM:(hd + 1) * V_DIM, :], ones], axis=0)
        acc = alpha * acc + jnp.dot(va, p, preferred_element_type=F32)
        return m_new, acc

    def finalize(hd, acc):
        on = acc[:V_DIM] / acc[V_DIM:V_DIM + 1]
        ot = on[:, :tq] - lam * on[:, tq:]
        ot = ot * lax.rsqrt(jnp.mean(ot * ot, axis=0, keepdims=True) + LN_EPS)
        o_ref[:, hd * V_DIM:(hd + 1) * V_DIM] = (ot.T * g_ref[...] * (1.0 - lambda_init)).astype(BF16)

    n_trips = (n_chunks - 2) // CHUNKS_PER_TRIP
    qqt = queries(0)
    mx_a = scores(0, qqt, 0, sa_ref)
    for hd in range(N_ATTN_HEADS):
        def pair(j, mx_a, state, hd=hd, qqt=qqt):
            mx_b = scores(hd, qqt, j + 1, sb_ref)
            state = accumulate(hd, j, sa_ref, mx_a, state)
            mx_a = scores(hd, qqt, j + 2, sa_ref)
            state = accumulate(hd, j + 1, sb_ref, mx_b, state)
            return mx_a, state

        def body(i, carry, pair=pair):
            for u in range(0, CHUNKS_PER_TRIP, 2):
                carry = pair(CHUNKS_PER_TRIP * i + u, *carry)
            return carry

        state = (jnp.full((1, 2 * tq), -jnp.inf, F32), jnp.zeros((V_DIM + DENOM_ROWS, 2 * tq), F32))
        carry = lax.fori_loop(0, n_trips, body, (mx_a, state))
        for j in range(n_trips * CHUNKS_PER_TRIP, n_chunks - 2, 2):
            carry = pair(j, *carry)
        mx_a, state = carry
        mx_b = scores(hd, qqt, n_chunks - 1, sb_ref)
        state = accumulate(hd, n_chunks - 2, sa_ref, mx_a, state)
        if hd + 1 < N_ATTN_HEADS:
            qqt = queries(hd + 1)
            mx_a = scores(hd + 1, qqt, 0, sa_ref)
        _, acc = accumulate(hd, n_chunks - 1, sb_ref, mx_b, state)
        finalize(hd, acc)


def _attention(qt, k, vt, lam_vecs, subln_g, lambda_init, tq=512):
    b, s, _ = k.shape
    n_chunks, tk = vt.shape[1], vt.shape[3]
    tq = min(tq, s)
    assert n_chunks % 2 == 0, "the chunk loop is pipelined over pairs"
    once = pl.Buffered(1)
    return pl.pallas_call(
        functools.partial(_attn_kernel, lambda_init=lambda_init),
        grid=(b, s // tq),
        in_specs=[pl.BlockSpec((4, QK_DIM), lambda bi, i: (0, 0)),
                  pl.BlockSpec((1, V_DIM), lambda bi, i: (0, 0)),
                  pl.BlockSpec((None, QK_WIDTH, tq), lambda bi, i: (bi, 0, i)),
                  pl.BlockSpec((None, s, QK_WIDTH), lambda bi, i: (bi, 0, 0), pipeline_mode=once),
                  pl.BlockSpec((None, n_chunks, ATTN_WIDTH, tk), lambda bi, i: (bi, 0, 0, 0),
                               pipeline_mode=once)],
        out_specs=pl.BlockSpec((None, tq, ATTN_WIDTH), lambda bi, i: (bi, i, 0)),
        out_shape=jax.ShapeDtypeStruct((b, s, ATTN_WIDTH), BF16),
        scratch_shapes=[pltpu.VMEM((tk, 2 * tq), F32), pltpu.VMEM((tk, 2 * tq), F32)],
        compiler_params=_params(2),
        name="attention",
    )(lam_vecs, subln_g, qt, k, vt)


def _dft_tables(n):
    idx = np.arange(n, dtype=np.int64)
    ang = (2.0 * np.pi / n) * ((idx[:, None] * idx[None, :]) % n)
    return np.cos(ang), np.sin(ang)


def _channel_table(seq):
    c, s = _dft_tables(FOURIER_GROUP_DIM)
    scale = 1.0 / math.sqrt(seq * FOURIER_GROUP_DIM)
    return jnp.asarray(np.concatenate([c, -s], axis=0) * scale, F32).astype(BF16)


def _fourier_direct_kernel(u_ref, cs_ref, ch_ref, wf_ref, o_ref):
    u = u_ref[...].astype(BF16)
    pr = jnp.dot(cs_ref[0], u, preferred_element_type=F32)
    qi = jnp.dot(cs_ref[1], u, preferred_element_type=F32)
    ch = ch_ref[...]
    for g in range(N_FOURIER_GROUPS):
        sl = slice(g * LANES, (g + 1) * LANES)
        y = jnp.concatenate([pr[:, sl], qi[:, sl]], axis=1).astype(BF16)
        f = jnp.dot(y, ch, preferred_element_type=F32)
        o_ref[:, sl] = jnp.dot(f.astype(BF16), wf_ref[g].astype(BF16),
                               preferred_element_type=F32).astype(o_ref.dtype)


def _fourier_direct(u, w_f, tt=512):
    b, s, _ = u.shape
    tt = min(tt, s)
    c, sn = _dft_tables(s)
    cs = jnp.asarray(np.stack([c, sn]), F32).astype(BF16)
    return pl.pallas_call(
        _fourier_direct_kernel,
        grid=(s // tt, b),
        in_specs=[pl.BlockSpec((None, s, FOURIER_WIDTH), lambda i, bi: (bi, 0, 0)),
                  pl.BlockSpec((2, tt, s), lambda i, bi: (0, i, 0)),
                  pl.BlockSpec((2 * FOURIER_GROUP_DIM, FOURIER_GROUP_DIM), lambda i, bi: (0, 0)),
                  pl.BlockSpec(w_f.shape, lambda i, bi: (0, 0, 0))],
        out_specs=pl.BlockSpec((None, tt, FOURIER_WIDTH), lambda i, bi: (bi, i, 0)),
        out_shape=jax.ShapeDtypeStruct((b, s, FOURIER_WIDTH), BF16),
        compiler_params=_params(2),
        name="fourier_direct",
    )(u, cs, _channel_table(s), w_f)


def _fourier_two_stage_kernel(u_ref, m1_ref, tw_ref, m2_ref, ch_ref, wf_ref, o_ref,
                              er_ref, em_ref, *, n1, n2, unroll):
    m1 = m1_ref[...]
    cb = tw_ref[0]
    sb = tw_ref[1]
    ab = jnp.dot(ch_ref[...], wf_ref[...].astype(BF16), preferred_element_type=F32)
    a_c = ab[:FOURIER_GROUP_DIM]
    nb_c = ab[FOURIER_GROUP_DIM:]
    mch = jnp.concatenate([jnp.concatenate([a_c, -nb_c], axis=1),
                           jnp.concatenate([nb_c, a_c], axis=1)], axis=0).astype(BF16)

    def stage1(s2, carry):
        cw, sw = carry
        us = u_ref[pl.ds(s2, n1, stride=n2), :].astype(BF16)
        gh = jnp.dot(m1, us, preferred_element_type=F32)
        g = gh[:n1]
        h = gh[n1:]
        gt = jnp.concatenate([g * cw - h * sw, g * sw + h * cw], axis=1).astype(BF16)
        e = jnp.dot(gt, mch, preferred_element_type=F32)
        row0 = pl.multiple_of(s2 * n1, n1)
        er_ref[pl.ds(row0, n1), :] = e[:, :LANES]
        em_ref[pl.ds(row0, n1), :] = e[:, LANES:]
        return cw * cb - sw * sb, sw * cb + cw * sb

    lax.fori_loop(0, n2, stage1, (jnp.ones_like(cb), jnp.zeros_like(sb)), unroll=unroll)

    m2 = m2_ref[...]

    def stage2(t1, carry):
        x = jnp.concatenate([er_ref[pl.ds(t1, n2, stride=n1), :],
                             em_ref[pl.ds(t1, n2, stride=n1), :]], axis=0).astype(BF16)
        o_ref[pl.ds(t1, n2, stride=n1), :] = jnp.dot(m2, x, preferred_element_type=F32)
        return carry

    lax.fori_loop(0, n1, stage2, 0, unroll=unroll)


def _fourier_two_stage(u, w_f, n1=DFT_STAGE1, unroll=4):
    b, s, _ = u.shape
    n2 = s // n1
    c1, s1 = _dft_tables(n1)
    c2, s2 = _dft_tables(n2)
    m1 = jnp.asarray(np.concatenate([c1, s1], axis=0), F32).astype(BF16)
    m2 = jnp.asarray(np.concatenate([c2, -s2], axis=1), F32).astype(BF16)
    ang = (2.0 * np.pi / s) * np.arange(n1, dtype=np.float64)
    tw = np.stack([np.broadcast_to(np.cos(ang)[:, None], (n1, LANES)),
                   np.broadcast_to(np.sin(ang)[:, None], (n1, LANES))])
    tw = jnp.asarray(tw, F32)
    grp = lambda bi, g: (bi, 0, g)
    return pl.pallas_call(
        functools.partial(_fourier_two_stage_kernel, n1=n1, n2=n2, unroll=unroll),
        grid=(b, N_FOURIER_GROUPS),
        in_specs=[pl.BlockSpec((None, s, LANES), grp, pipeline_mode=pl.Buffered(1)),
                  pl.BlockSpec(m1.shape, lambda bi, g: (0, 0)),
                  pl.BlockSpec(tw.shape, lambda bi, g: (0, 0, 0)),
                  pl.BlockSpec(m2.shape, lambda bi, g: (0, 0)),
                  pl.BlockSpec((2 * FOURIER_GROUP_DIM, FOURIER_GROUP_DIM), lambda bi, g: (0, 0)),
                  pl.BlockSpec((None, FOURIER_GROUP_DIM, FOURIER_GROUP_DIM), lambda bi, g: (g, 0, 0))],
        out_specs=pl.BlockSpec((None, s, LANES), grp),
        out_shape=jax.ShapeDtypeStruct((b, s, FOURIER_WIDTH), F32),
        scratch_shapes=[pltpu.VMEM((s, LANES), F32), pltpu.VMEM((s, LANES), F32)],
        compiler_params=_params(2),
        name="fourier_two_stage",
    )(u, m1, tw, m2, _channel_table(s), w_f)


def _fourier(u, w_f):
    if u.shape[1] <= DIRECT_DFT_MAX_SEQ:
        return _fourier_direct(u, w_f)
    return _fourier_two_stage(u, w_f)


def _ffn_kernel(x_ref, o_ref, f_ref, mod_ref, wo_ref, ln1g_ref, ln1b_ref,
                wg_ref, wu_ref, wd_ref, ln2g_ref, ln2b_ref, y_ref, *, n_sub):
    g1 = mod_ref[2:3, :]
    sh2 = mod_ref[3:4, :]
    sc2 = mod_ref[4:5, :]
    g2 = mod_ref[5:6, :]
    rows = x_ref.shape[0] // n_sub
    subs = [slice(r * rows, (r + 1) * rows) for r in range(n_sub)]
    hs, ffns = [], []
    for sl in subs:
        mix = (jnp.dot(o_ref[sl, :], wo_ref[:ATTN_WIDTH, :], preferred_element_type=F32)
               + jnp.dot(f_ref[sl, :].astype(BF16), wo_ref[ATTN_WIDTH:, :], preferred_element_type=F32))
        x1 = _layer_norm(ALPHA * x_ref[sl, :] + (1.0 + g1) * mix) * ln1g_ref[...] + ln1b_ref[...]
        y_ref[sl, :] = x1
        hs.append((_layer_norm(x1) * (1.0 + sc2) + sh2).astype(BF16))
    for h in hs:
        gate = jnp.dot(h, wg_ref[...], preferred_element_type=F32)
        up = jnp.dot(h, wu_ref[...], preferred_element_type=F32)
        a = (_silu(gate) * up).astype(BF16)
        ffns.append(jnp.dot(a, wd_ref[...], preferred_element_type=F32))
    for sl, ffn in zip(subs, ffns):
        y_ref[sl, :] = _layer_norm(ALPHA * y_ref[sl, :] + (1.0 + g2) * ffn) * ln2g_ref[...] + ln2b_ref[...]


def _ffn(x, o, f, mod, wo, ln1g, ln1b, wg, wu, wd, ln2g, ln2b, tm=512, n_sub=2):
    b, s, d = x.shape
    tm = min(tm, s)
    tok = lambda bi, i: (bi, i, 0)
    const = lambda bi, i: (0, 0)
    resident = lambda shape: pl.BlockSpec(shape, const, pipeline_mode=pl.Buffered(1))
    return pl.pallas_call(
        functools.partial(_ffn_kernel, n_sub=n_sub),
        grid=(b, s // tm),
        in_specs=[pl.BlockSpec((None, tm, d), tok),
                  pl.BlockSpec((None, tm, ATTN_WIDTH), tok),
                  pl.BlockSpec((None, tm, FOURIER_WIDTH), tok),
                  pl.BlockSpec((None, N_MOD, d), lambda bi, i: (bi, 0, 0)),
                  resident(wo.shape), resident((1, d)), resident((1, d)),
                  resident(wg.shape), resident(wu.shape), resident(wd.shape),
                  resident((1, d)), resident((1, d))],
        out_specs=pl.BlockSpec((None, tm, d), tok),
        out_shape=jax.ShapeDtypeStruct((b, s, d), F32),
        compiler_params=_params(2),
        name="ffn",
    )(x, o, f, mod, wo, ln1g, ln1b, wg, wu, wd, ln2g, ln2b)


def _rope_tables(seq):
    inv = ROPE_THETA ** (-jnp.arange(0, QK_DIM, 2, dtype=F32) / QK_DIM)
    ang = jnp.arange(seq, dtype=F32)[:, None] * inv[None, :]
    cos = jnp.cos(ang)
    sin = jnp.sin(ang)
    reps = LANES // (QK_DIM // 2)
    sign = jnp.tile(jnp.concatenate([-jnp.ones((QK_DIM // 2,), F32), jnp.ones((QK_DIM // 2,), F32)]),
                    LANES // QK_DIM)
    return jnp.tile(cos, (1, reps)), jnp.tile(sin, (1, reps)) * sign, cos.T, sin.T


def _layer(x, mod, l, w_qv_t, w_ku, lam_vecs, subln_g, w_fourier, w_out, ln1_g, ln1_b,
           w_gate, w_up, w_down, ln2_g, ln2_b):
    lambda_init = 0.8 - 0.6 * math.exp(-0.3 * l)
    s = x.shape[1]
    qt, k, vt, u = _inproj(x, mod, w_qv_t, w_ku, _rope_tables(s), tm=min(KV_CHUNK, s))
    o = _attention(qt, k, vt, lam_vecs, subln_g, lambda_init)
    f = _fourier(u, w_fourier)
    return _ffn(x, o, f, mod, w_out, ln1_g, ln1_b, w_gate, w_up, w_down, ln2_g, ln2_b)


def _split_w_in(w_in):
    q, k, v, u = jnp.split(w_in, [QK_WIDTH, 2 * QK_WIDTH, 2 * QK_WIDTH + ATTN_WIDTH], axis=1)
    return (jnp.concatenate([q, v], axis=1).T.astype(BF16), jnp.concatenate([k, u], axis=1).astype(BF16))


def kernel(x_prompt, x_sample, c_prompt, c_sample, w_ada, b_ada, w_in, lambda_q1, lambda_k1, lambda_q2, lambda_k2, subln_g, w_fourier, w_out, ln1_g, ln1_b, w_gate, w_up, w_down, ln2_g, ln2_b):
    n_prompt = c_prompt.shape[0]
    c_all = jnp.concatenate([c_prompt, c_sample], axis=0)
    y_prompt, y_sample = x_prompt, x_sample
    for l in range(DEPTH):
        mod = _ada(c_all, w_ada[l], b_ada[l][None, :]).reshape(c_all.shape[0], N_MOD, D_MODEL)
        lam_vecs = jnp.stack([lambda_q1[l], lambda_k1[l], lambda_q2[l], lambda_k2[l]]).astype(F32)
        weights = (*_split_w_in(w_in[l]), lam_vecs, subln_g[l][None, :], w_fourier[l],
                   w_out[l].astype(BF16), ln1_g[l][None, :], ln1_b[l][None, :],
                   w_gate[l].astype(BF16), w_up[l].astype(BF16), w_down[l].astype(BF16),
                   ln2_g[l][None, :], ln2_b[l][None, :])
        y_prompt = _layer(y_prompt, mod[:n_prompt], l, *weights)
        y_sample = _layer(y_sample, mod[n_prompt:], l, *weights)
    return (y_prompt, y_sample)
```

```python
import functools
import math

import numpy as np
import jax
import jax.numpy as jnp
from jax import lax
from jax.experimental import pallas as pl
from jax.experimental.pallas import tpu as pltpu

D_MODEL = 1024
DEPTH = 1
N_ATTN_HEADS = 4
QK_DIM = 64
V_DIM = 2 * QK_DIM
ATTN_WIDTH = N_ATTN_HEADS * V_DIM
QK_WIDTH = N_ATTN_HEADS * 2 * QK_DIM
N_FOURIER_GROUPS = 4
FOURIER_GROUP_DIM = 128
FOURIER_WIDTH = N_FOURIER_GROUPS * FOURIER_GROUP_DIM
IN_WIDTH = 2 * QK_WIDTH + ATTN_WIDTH + FOURIER_WIDTH
D_FF = -(-8 * D_MODEL // (3 * 256)) * 256
ROPE_THETA = 10000.0
LN_EPS = 1e-5
ALPHA = (2 * DEPTH) ** 0.25
N_MOD = 6

LANES = 128
VMEM_LIMIT = 56 * 1024 * 1024
DFT_STAGE1 = 128
S2_BLOCK = 16
KV_CHUNK = 512
CHUNKS_PER_TRIP = 10
DENOM_ROWS = 16

F32 = jnp.float32
BF16 = jnp.bfloat16


def _params(n_grid_dims):
    return pltpu.CompilerParams(
        dimension_semantics=("arbitrary",) * n_grid_dims, vmem_limit_bytes=VMEM_LIMIT)


def _layer_norm(x):
    mu = jnp.mean(x, axis=-1, keepdims=True)
    xc = x - mu
    var = jnp.mean(xc * xc, axis=-1, keepdims=True)
    return xc * lax.rsqrt(var + LN_EPS)


def _silu(x):
    return x * (1.0 / (1.0 + jnp.exp(-x)))


def _ada_kernel(c_ref, w_ref, b_ref, o_ref):
    s = _silu(c_ref[...])
    o_ref[...] = jnp.dot(s, w_ref[...], preferred_element_type=F32,
                         precision=lax.Precision.HIGHEST) + b_ref[...]


def _ada(c, w, b, tn=1536):
    n_rows, d = c.shape
    n_out = w.shape[1]
    return pl.pallas_call(
        _ada_kernel,
        grid=(n_out // tn,),
        in_specs=[pl.BlockSpec((n_rows, d), lambda j: (0, 0)),
                  pl.BlockSpec((d, tn), lambda j: (0, j)),
                  pl.BlockSpec((1, tn), lambda j: (0, j))],
        out_specs=pl.BlockSpec((n_rows, tn), lambda j: (0, j)),
        out_shape=jax.ShapeDtypeStruct((n_rows, n_out), F32),
        compiler_params=_params(1),
        name="ada",
    )(c, w, b)


def _inproj_kernel(x_ref, mod_ref, wqv_ref, wku_ref, cos_ref, sin_ref, cost_ref, sint_ref,
                   qt_ref, k_ref, vt_ref, u_ref):
    h = (_layer_norm(x_ref[...]) * (1.0 + mod_ref[1:2, :]) + mod_ref[0:1, :]).astype(BF16)
    qvt = lax.dot_general(wqv_ref[...], h, (((1,), (1,)), ((), ())), preferred_element_type=F32)
    cost = cost_ref[...]
    sint = sint_ref[...]
    half = QK_DIM // 2
    scale = QK_DIM ** -0.5 * math.log2(math.e)
    for c in range(QK_WIDTH // QK_DIM):
        a = qvt[c * QK_DIM: c * QK_DIM + half, :]
        b = qvt[c * QK_DIM + half: (c + 1) * QK_DIM, :]
        qt_ref[c * QK_DIM: c * QK_DIM + half, :] = ((a * cost - b * sint) * scale).astype(BF16)
        qt_ref[c * QK_DIM + half: (c + 1) * QK_DIM, :] = ((b * cost + a * sint) * scale).astype(BF16)
    vt_ref[...] = qvt[QK_WIDTH:, :].astype(BF16)

    ku = jnp.dot(h, wku_ref[...], preferred_element_type=F32)
    cos = cos_ref[...]
    sin = sin_ref[...]
    lane = lax.broadcasted_iota(jnp.int32, cos.shape, 1)
    first_half = (lane % QK_DIM) < half
    for hd in range(N_ATTN_HEADS):
        xh = ku[:, hd * LANES:(hd + 1) * LANES]
        partner = jnp.where(first_half, pltpu.roll(xh, LANES - half, 1), pltpu.roll(xh, half, 1))
        k_ref[:, hd * LANES:(hd + 1) * LANES] = (xh * cos + partner * sin).astype(BF16)
    for g in range(N_FOURIER_GROUPS):
        u_ref[g] = ku[:, QK_WIDTH + g * LANES: QK_WIDTH + (g + 1) * LANES].astype(BF16)


def _inproj(x, mod, w_qv_t, w_ku, rope, tm):
    b, s, d = x.shape
    cos_t, sin_t, cos_tt, sin_tt = rope
    tok = lambda bi, i: (bi, i, 0)
    const = lambda bi, i: (0, 0)
    return pl.pallas_call(
        _inproj_kernel,
        grid=(b, s // tm),
        in_specs=[pl.BlockSpec((None, tm, d), tok),
                  pl.BlockSpec((None, N_MOD, d), lambda bi, i: (bi, 0, 0)),
                  pl.BlockSpec(w_qv_t.shape, const),
                  pl.BlockSpec(w_ku.shape, const),
                  pl.BlockSpec((tm, LANES), lambda bi, i: (i, 0)),
                  pl.BlockSpec((tm, LANES), lambda bi, i: (i, 0)),
                  pl.BlockSpec((QK_DIM // 2, tm), lambda bi, i: (0, i)),
                  pl.BlockSpec((QK_DIM // 2, tm), lambda bi, i: (0, i))],
        out_specs=[pl.BlockSpec((None, QK_WIDTH, tm), lambda bi, i: (bi, 0, i)),
                   pl.BlockSpec((None, tm, QK_WIDTH), tok),
                   pl.BlockSpec((None, None, ATTN_WIDTH, tm), lambda bi, i: (bi, i, 0, 0)),
                   pl.BlockSpec((None, N_FOURIER_GROUPS, tm, LANES), lambda bi, i: (bi, 0, i, 0))],
        out_shape=[jax.ShapeDtypeStruct((b, QK_WIDTH, s), BF16),
                   jax.ShapeDtypeStruct((b, s, QK_WIDTH), BF16),
                   jax.ShapeDtypeStruct((b, s // tm, ATTN_WIDTH, tm), BF16),
                   jax.ShapeDtypeStruct((b, N_FOURIER_GROUPS, s, LANES), BF16)],
        compiler_params=_params(2),
        name="inproj",
    )(x, mod, w_qv_t, w_ku, cos_t, sin_t, cos_tt, sin_tt)


def _attn_kernel(lam_ref, g_ref, qt_ref, k_ref, vt_ref, o_ref, sa_ref, sb_ref, *, lambda_init):
    tq = qt_ref.shape[1]
    n_chunks, _, tk = vt_ref.shape
    row = lax.broadcasted_iota(jnp.int32, (LANES, tq), 0)
    ones = jnp.ones((DENOM_ROWS, tk), BF16)
    lam = (jnp.exp(jnp.sum(lam_ref[0:1, :] * lam_ref[1:2, :], axis=-1, keepdims=True))
           - jnp.exp(jnp.sum(lam_ref[2:3, :] * lam_ref[3:4, :], axis=-1, keepdims=True))
           + lambda_init)

    def queries(hd):
        qt = qt_ref[hd * LANES:(hd + 1) * LANES, :]
        zero = jnp.zeros_like(qt)
        return jnp.concatenate([jnp.where(row < QK_DIM, qt, zero),
                                jnp.where(row >= QK_DIM, qt, zero)], axis=1)

    def scores(hd, qqt, j, s_ref):
        start = pl.multiple_of(j * tk, tk)
        kc = k_ref[pl.ds(start, tk), hd * LANES:(hd + 1) * LANES]
        s = jnp.dot(kc, qqt, preferred_element_type=F32)
        s_ref[...] = s
        return jnp.max(s, axis=0, keepdims=True)

    def accumulate(hd, j, s_ref, mx, state):
        m, acc = state
        m_new = jnp.maximum(m, mx)
        alpha = jnp.exp2(m - m_new)
        p = jnp.exp2(s_ref[...] - m_new).astype(BF16)
        va = jnp.concatenate([vt_ref[j, hd * V_DIM:(hd + 1) * V_DIM, :], ones], axis=0)
        acc = alpha * acc + jnp.dot(va, p, preferred_element_type=F32)
        return m_new, acc

    def finalize(hd, acc):
        on = acc[:V_DIM] / acc[V_DIM:V_DIM + 1]
        ot = on[:, :tq] - lam * on[:, tq:]
        ot = ot * lax.rsqrt(jnp.mean(ot * ot, axis=0, keepdims=True) + LN_EPS)
        o_ref[:, hd * V_DIM:(hd + 1) * V_DIM] = (ot.T * g_ref[...] * (1.0 - lambda_init)).astype(BF16)

    n_trips = (n_chunks - 2) // CHUNKS_PER_TRIP
    qqt = queries(0)
    mx_a = scores(0, qqt, 0, sa_ref)
    for hd in range(N_ATTN_HEADS):
        def pair(j, mx_a, state, hd=hd, qqt=qqt):
            mx_b = scores(hd, qqt, j + 1, sb_ref)
            state = accumulate(hd, j, sa_ref, mx_a, state)
            mx_a = scores(hd, qqt, j + 2, sa_ref)
            state = accumulate(hd, j + 1, sb_ref, mx_b, state)
            return mx_a, state

        def body(i, carry, pair=pair):
            for u in range(0, CHUNKS_PER_TRIP, 2):
                carry = pair(CHUNKS_PER_TRIP * i + u, *carry)
            return carry

        state = (jnp.full((1, 2 * tq), -jnp.inf, F32), jnp.zeros((V_DIM + DENOM_ROWS, 2 * tq), F32))
        carry = lax.fori_loop(0, n_trips, body, (mx_a, state))
        for j in range(n_trips * CHUNKS_PER_TRIP, n_chunks - 2, 2):
            carry = pair(j, *carry)
        mx_a, state = carry
        mx_b = scores(hd, qqt, n_chunks - 1, sb_ref)
        state = accumulate(hd, n_chunks - 2, sa_ref, mx_a, state)
        if hd + 1 < N_ATTN_HEADS:
            qqt = queries(hd + 1)
            mx_a = scores(hd + 1, qqt, 0, sa_ref)
        _, acc = accumulate(hd, n_chunks - 1, sb_ref, mx_b, state)
        finalize(hd, acc)


def _attention(qt, k, vt, lam_vecs, subln_g, lambda_init, tq=512):
    b, s, _ = k.shape
    n_chunks, tk = vt.shape[1], vt.shape[3]
    tq = min(tq, s)
    assert n_chunks % 2 == 0, "the chunk loop is pipelined over pairs"
    once = pl.Buffered(1)
    return pl.pallas_call(
        functools.partial(_attn_kernel, lambda_init=lambda_init),
        grid=(b, s // tq),
        in_specs=[pl.BlockSpec((4, QK_DIM), lambda bi, i: (0, 0)),
                  pl.BlockSpec((1, V_DIM), lambda bi, i: (0, 0)),
                  pl.BlockSpec((None, QK_WIDTH, tq), lambda bi, i: (bi, 0, i)),
                  pl.BlockSpec((None, s, QK_WIDTH), lambda bi, i: (bi, 0, 0), pipeline_mode=once),
                  pl.BlockSpec((None, n_chunks, ATTN_WIDTH, tk), lambda bi, i: (bi, 0, 0, 0),
                               pipeline_mode=once)],
        out_specs=pl.BlockSpec((None, tq, ATTN_WIDTH), lambda bi, i: (bi, i, 0)),
        out_shape=jax.ShapeDtypeStruct((b, s, ATTN_WIDTH), BF16),
        scratch_shapes=[pltpu.VMEM((tk, 2 * tq), F32), pltpu.VMEM((tk, 2 * tq), F32)],
        compiler_params=_params(2),
        name="attention",
    )(lam_vecs, subln_g, qt, k, vt)


def _angles(prod, n):
    ang = (2.0 * np.pi / n) * (prod % n)
    return np.cos(ang), np.sin(ang)


def _dft_tables(n):
    idx = np.arange(n, dtype=np.int64)
    return _angles(idx[:, None] * idx[None, :], n)


def _twiddle_tables(seq, n1, n2):
    t1 = np.arange(n1, dtype=np.int64)

    def table(mult):
        c, s = _angles(mult[:, None] * t1[None, :], seq)
        return jnp.asarray(np.broadcast_to(np.stack([c, s], axis=1)[..., None],
                                           (len(mult), 2, n1, LANES)), F32)

    return (table(S2_BLOCK * np.arange(n2 // S2_BLOCK, dtype=np.int64)),
            table(np.arange(S2_BLOCK, dtype=np.int64)))


def _fourier_stage1_kernel(z_ref, m1_ref, wb_ref, w8_ref, ch_ref, wf_ref, er_ref, em_ref):
    n1 = z_ref.shape[0]
    m1 = m1_ref[...]
    ab = jnp.dot(ch_ref[...], wf_ref[...].astype(BF16), preferred_element_type=F32)
    a_c = ab[:FOURIER_GROUP_DIM]
    nb_c = ab[FOURIER_GROUP_DIM:]
    mch = jnp.concatenate([jnp.concatenate([a_c, -nb_c], axis=1),
                           jnp.concatenate([nb_c, a_c], axis=1)], axis=0).astype(BF16)
    cb = wb_ref[0]
    sb = wb_ref[1]
    for c in range(S2_BLOCK):
        gh = jnp.dot(m1, z_ref[:, c * LANES:(c + 1) * LANES], preferred_element_type=F32)
        g = gh[:n1]
        h = gh[n1:]
        cw = cb * w8_ref[c, 0] - sb * w8_ref[c, 1]
        sw = sb * w8_ref[c, 0] + cb * w8_ref[c, 1]
        gt = jnp.concatenate([g * cw - h * sw, g * sw + h * cw], axis=1).astype(BF16)
        e = jnp.dot(gt, mch, preferred_element_type=F32)
        er_ref[c] = e[:, :LANES].astype(BF16)
        em_ref[c] = e[:, LANES:].astype(BF16)


def _fourier_stage2_kernel(er_ref, em_ref, m2_ref, o_ref):
    x = jnp.concatenate([er_ref[...], em_ref[...]], axis=0)
    o_ref[...] = jnp.dot(m2_ref[...], x, preferred_element_type=F32).astype(BF16)


def _fourier(u, w_f, n1=DFT_STAGE1, stage2_block_elems=256 * 1024):
    b, n_g, s, _ = u.shape
    n2 = s // n1
    assert n2 % S2_BLOCK == 0
    c1, s1 = _dft_tables(n1)
    c2, s2 = _dft_tables(n2)
    cc, sc = _dft_tables(FOURIER_GROUP_DIM)
    m1 = jnp.asarray(np.concatenate([c1, s1], axis=0), F32).astype(BF16)
    m2 = jnp.asarray(np.concatenate([c2, -s2], axis=1), F32).astype(BF16)
    ch = jnp.asarray(np.concatenate([cc, -sc], axis=0) / math.sqrt(s * FOURIER_GROUP_DIM), F32).astype(BF16)
    wb, w8 = _twiddle_tables(s, n1, n2)
    const2 = lambda bi, g, j: (0, 0)
    e_shape = jax.ShapeDtypeStruct((b, n_g, n2, n1, LANES), BF16)
    e_block = pl.BlockSpec((None, None, S2_BLOCK, n1, LANES), lambda bi, g, j: (bi, g, j, 0, 0))
    er, em = pl.pallas_call(
        _fourier_stage1_kernel,
        grid=(b, n_g, n2 // S2_BLOCK),
        in_specs=[pl.BlockSpec((None, None, n1, S2_BLOCK * LANES), lambda bi, g, j: (bi, g, 0, j)),
                  pl.BlockSpec(m1.shape, const2),
                  pl.BlockSpec((None, 2, n1, LANES), lambda bi, g, j: (j, 0, 0, 0)),
                  pl.BlockSpec(w8.shape, lambda bi, g, j: (0, 0, 0, 0)),
                  pl.BlockSpec(ch.shape, const2),
                  pl.BlockSpec((None, FOURIER_GROUP_DIM, FOURIER_GROUP_DIM), lambda bi, g, j: (g, 0, 0))],
        out_specs=[e_block, e_block],
        out_shape=[e_shape, e_shape],
        compiler_params=_params(3),
        name="fourier_stage1",
    )(u.reshape(b, n_g, n1, n2 * LANES), m1, wb, w8, ch, w_f)
    tn = min(stage2_block_elems // n2, n1 * LANES)
    flat = pl.BlockSpec((None, None, n2, tn), lambda bi, g, j: (bi, g, 0, j))
    y = pl.pallas_call(
        _fourier_stage2_kernel,
        grid=(b, n_g, n1 * LANES // tn),
        in_specs=[flat, flat, pl.BlockSpec(m2.shape, const2)],
        out_specs=flat,
        out_shape=jax.ShapeDtypeStruct((b, n_g, n2, n1 * LANES), BF16),
        compiler_params=_params(3),
        name="fourier_stage2",
    )(er.reshape(b, n_g, n2, n1 * LANES), em.reshape(b, n_g, n2, n1 * LANES), m2)
    return y.reshape(b, n_g, s, LANES)


def _ffn_kernel(x_ref, o_ref, f_ref, mod_ref, wo_ref, ln1g_ref, ln1b_ref,
                wg_ref, wu_ref, wd_ref, ln2g_ref, ln2b_ref, y_ref, *, n_sub):
    g1 = mod_ref[2:3, :]
    sh2 = mod_ref[3:4, :]
    sc2 = mod_ref[4:5, :]
    g2 = mod_ref[5:6, :]
    rows = x_ref.shape[0] // n_sub
    subs = [slice(r * rows, (r + 1) * rows) for r in range(n_sub)]
    hs, ffns = [], []
    for sl in subs:
        mix = (jnp.dot(o_ref[sl, :], wo_ref[:ATTN_WIDTH, :], preferred_element_type=F32)
               + jnp.dot(jnp.concatenate([f_ref[g, sl, :] for g in range(N_FOURIER_GROUPS)], axis=1),
                         wo_ref[ATTN_WIDTH:, :], preferred_element_type=F32))
        x1 = _layer_norm(ALPHA * x_ref[sl, :] + (1.0 + g1) * mix) * ln1g_ref[...] + ln1b_ref[...]
        y_ref[sl, :] = x1
        hs.append((_layer_norm(x1) * (1.0 + sc2) + sh2).astype(BF16))
    for h in hs:
        gate = jnp.dot(h, wg_ref[...], preferred_element_type=F32)
        up = jnp.dot(h, wu_ref[...], preferred_element_type=F32)
        a = (_silu(gate) * up).astype(BF16)
        ffns.append(jnp.dot(a, wd_ref[...], preferred_element_type=F32))
    for sl, ffn in zip(subs, ffns):
        y_ref[sl, :] = _layer_norm(ALPHA * y_ref[sl, :] + (1.0 + g2) * ffn) * ln2g_ref[...] + ln2b_ref[...]


def _ffn(x, o, f, mod, wo, ln1g, ln1b, wg, wu, wd, ln2g, ln2b, tm=512, n_sub=2):
    b, s, d = x.shape
    tm = min(tm, s)
    tok = lambda bi, i: (bi, i, 0)
    const = lambda bi, i: (0, 0)
    resident = lambda shape: pl.BlockSpec(shape, const, pipeline_mode=pl.Buffered(1))
    return pl.pallas_call(
        functools.partial(_ffn_kernel, n_sub=n_sub),
        grid=(b, s // tm),
        in_specs=[pl.BlockSpec((None, tm, d), tok),
                  pl.BlockSpec((None, tm, ATTN_WIDTH), tok),
                  pl.BlockSpec((None, N_FOURIER_GROUPS, tm, LANES), lambda bi, i: (bi, 0, i, 0)),
                  pl.BlockSpec((None, N_MOD, d), lambda bi, i: (bi, 0, 0)),
                  resident(wo.shape), resident((1, d)), resident((1, d)),
                  resident(wg.shape), resident(wu.shape), resident(wd.shape),
                  resident((1, d)), resident((1, d))],
        out_specs=pl.BlockSpec((None, tm, d), tok),
        out_shape=jax.ShapeDtypeStruct((b, s, d), F32),
        compiler_params=_params(2),
        name="ffn",
    )(x, o, f, mod, wo, ln1g, ln1b, wg, wu, wd, ln2g, ln2b)


def _rope_tables(seq):
    inv = ROPE_THETA ** (-jnp.arange(0, QK_DIM, 2, dtype=F32) / QK_DIM)
    ang = jnp.arange(seq, dtype=F32)[:, None] * inv[None, :]
    cos = jnp.cos(ang)
    sin = jnp.sin(ang)
    reps = LANES // (QK_DIM // 2)
    sign = jnp.tile(jnp.concatenate([-jnp.ones((QK_DIM // 2,), F32), jnp.ones((QK_DIM // 2,), F32)]),
                    LANES // QK_DIM)
    return jnp.tile(cos, (1, reps)), jnp.tile(sin, (1, reps)) * sign, cos.T, sin.T


def _layer(x, mod, l, w_qv_t, w_ku, lam_vecs, subln_g, w_fourier, w_out, ln1_g, ln1_b,
           w_gate, w_up, w_down, ln2_g, ln2_b):
    lambda_init = 0.8 - 0.6 * math.exp(-0.3 * l)
    s = x.shape[1]
    qt, k, vt, u = _inproj(x, mod, w_qv_t, w_ku, _rope_tables(s), tm=min(KV_CHUNK, s))
    o = _attention(qt, k, vt, lam_vecs, subln_g, lambda_init)
    f = _fourier(u, w_fourier)
    return _ffn(x, o, f, mod, w_out, ln1_g, ln1_b, w_gate, w_up, w_down, ln2_g, ln2_b)


def _split_w_in(w_in):
    q, k, v, u = jnp.split(w_in, [QK_WIDTH, 2 * QK_WIDTH, 2 * QK_WIDTH + ATTN_WIDTH], axis=1)
    return (jnp.concatenate([q, v], axis=1).T.astype(BF16), jnp.concatenate([k, u], axis=1).astype(BF16))


def kernel(x_prompt, x_sample, c_prompt, c_sample, w_ada, b_ada, w_in, lambda_q1, lambda_k1, lambda_q2, lambda_k2, subln_g, w_fourier, w_out, ln1_g, ln1_b, w_gate, w_up, w_down, ln2_g, ln2_b):
    n_prompt = c_prompt.shape[0]
    c_all = jnp.concatenate([c_prompt, c_sample], axis=0)
    y_prompt, y_sample = x_prompt, x_sample
    for l in range(DEPTH):
        mod = _ada(c_all, w_ada[l], b_ada[l][None, :]).reshape(c_all.shape[0], N_MOD, D_MODEL)
        lam_vecs = jnp.stack([lambda_q1[l], lambda_k1[l], lambda_q2[l], lambda_k2[l]]).astype(F32)
        weights = (*_split_w_in(w_in[l]), lam_vecs, subln_g[l][None, :], w_fourier[l],
                   w_out[l].astype(BF16), ln1_g[l][None, :], ln1_b[l][None, :],
                   w_gate[l].astype(BF16), w_up[l].astype(BF16), w_down[l].astype(BF16),
                   ln2_g[l][None, :], ln2_b[l][None, :])
        y_prompt = _layer(y_prompt, mod[:n_prompt], l, *weights)
        y_sample = _layer(y_sample, mod[n_prompt:], l, *weights)
    return (y_prompt, y_sample)
```

```python
import functools
import math

import numpy as np
import jax
import jax.numpy as jnp
from jax import lax
from jax.experimental import pallas as pl
from jax.experimental.pallas import tpu as pltpu

D_MODEL = 1024
DEPTH = 1
N_ATTN_HEADS = 4
QK_DIM = 64
V_DIM = 2 * QK_DIM
ATTN_WIDTH = N_ATTN_HEADS * V_DIM
QK_WIDTH = N_ATTN_HEADS * 2 * QK_DIM
N_FOURIER_GROUPS = 4
FOURIER_GROUP_DIM = 128
FOURIER_WIDTH = N_FOURIER_GROUPS * FOURIER_GROUP_DIM
IN_WIDTH = 2 * QK_WIDTH + ATTN_WIDTH + FOURIER_WIDTH
D_FF = -(-8 * D_MODEL // (3 * 256)) * 256
ROPE_THETA = 10000.0
LN_EPS = 1e-5
ALPHA = (2 * DEPTH) ** 0.25
N_MOD = 6

LANES = 128
VMEM_LIMIT = 56 * 1024 * 1024
DIRECT_DFT_MAX_SEQ = 2048
DFT_STAGE1 = 128
KV_CHUNK = 512
CHUNKS_PER_TRIP = 10
DENOM_ROWS = 16

F32 = jnp.float32
BF16 = jnp.bfloat16


def _params(n_grid_dims):
    return pltpu.CompilerParams(
        dimension_semantics=("arbitrary",) * n_grid_dims, vmem_limit_bytes=VMEM_LIMIT)


def _layer_norm(x):
    mu = jnp.mean(x, axis=-1, keepdims=True)
    xc = x - mu
    var = jnp.mean(xc * xc, axis=-1, keepdims=True)
    return xc * lax.rsqrt(var + LN_EPS)


def _silu(x):
    return x * (1.0 / (1.0 + jnp.exp(-x)))


def _ada_kernel(c_ref, w_ref, b_ref, o_ref):
    s = _silu(c_ref[...])
    o_ref[...] = jnp.dot(s, w_ref[...], preferred_element_type=F32,
                         precision=lax.Precision.HIGHEST) + b_ref[...]


def _ada(c, w, b, tn=1536):
    n_rows, d = c.shape
    n_out = w.shape[1]
    return pl.pallas_call(
        _ada_kernel,
        grid=(n_out // tn,),
        in_specs=[pl.BlockSpec((n_rows, d), lambda j: (0, 0)),
                  pl.BlockSpec((d, tn), lambda j: (0, j)),
                  pl.BlockSpec((1, tn), lambda j: (0, j))],
        out_specs=pl.BlockSpec((n_rows, tn), lambda j: (0, j)),
        out_shape=jax.ShapeDtypeStruct((n_rows, n_out), F32),
        compiler_params=_params(1),
        name="ada",
    )(c, w, b)


def _inproj_kernel(x_ref, mod_ref, wqv_ref, wku_ref, cos_ref, sin_ref, cost_ref, sint_ref,
                   qt_ref, k_ref, vt_ref, u_ref):
    h = (_layer_norm(x_ref[...]) * (1.0 + mod_ref[1:2, :]) + mod_ref[0:1, :]).astype(BF16)
    qvt = lax.dot_general(wqv_ref[...], h, (((1,), (1,)), ((), ())), preferred_element_type=F32)
    cost = cost_ref[...]
    sint = sint_ref[...]
    half = QK_DIM // 2
    scale = QK_DIM ** -0.5 * math.log2(math.e)
    for c in range(QK_WIDTH // QK_DIM):
        a = qvt[c * QK_DIM: c * QK_DIM + half, :]
        b = qvt[c * QK_DIM + half: (c + 1) * QK_DIM, :]
        qt_ref[c * QK_DIM: c * QK_DIM + half, :] = ((a * cost - b * sint) * scale).astype(BF16)
        qt_ref[c * QK_DIM + half: (c + 1) * QK_DIM, :] = ((b * cost + a * sint) * scale).astype(BF16)
    vt_ref[...] = qvt[QK_WIDTH:, :].astype(BF16)

    ku = jnp.dot(h, wku_ref[...], preferred_element_type=F32)
    cos = cos_ref[...]
    sin = sin_ref[...]
    lane = lax.broadcasted_iota(jnp.int32, cos.shape, 1)
    first_half = (lane % QK_DIM) < half
    for hd in range(N_ATTN_HEADS):
        xh = ku[:, hd * LANES:(hd + 1) * LANES]
        partner = jnp.where(first_half, pltpu.roll(xh, LANES - half, 1), pltpu.roll(xh, half, 1))
        k_ref[:, hd * LANES:(hd + 1) * LANES] = (xh * cos + partner * sin).astype(BF16)
    u_ref[...] = ku[:, QK_WIDTH:]


def _inproj(x, mod, w_qv_t, w_ku, rope, tm):
    b, s, d = x.shape
    cos_t, sin_t, cos_tt, sin_tt = rope
    tok = lambda bi, i: (bi, i, 0)
    const = lambda bi, i: (0, 0)
    return pl.pallas_call(
        _inproj_kernel,
        grid=(b, s // tm),
        in_specs=[pl.BlockSpec((None, tm, d), tok),
                  pl.BlockSpec((None, N_MOD, d), lambda bi, i: (bi, 0, 0)),
                  pl.BlockSpec(w_qv_t.shape, const),
                  pl.BlockSpec(w_ku.shape, const),
                  pl.BlockSpec((tm, LANES), lambda bi, i: (i, 0)),
                  pl.BlockSpec((tm, LANES), lambda bi, i: (i, 0)),
                  pl.BlockSpec((QK_DIM // 2, tm), lambda bi, i: (0, i)),
                  pl.BlockSpec((QK_DIM // 2, tm), lambda bi, i: (0, i))],
        out_specs=[pl.BlockSpec((None, QK_WIDTH, tm), lambda bi, i: (bi, 0, i)),
                   pl.BlockSpec((None, tm, QK_WIDTH), tok),
                   pl.BlockSpec((None, None, ATTN_WIDTH, tm), lambda bi, i: (bi, i, 0, 0)),
                   pl.BlockSpec((None, tm, FOURIER_WIDTH), tok)],
        out_shape=[jax.ShapeDtypeStruct((b, QK_WIDTH, s), BF16),
                   jax.ShapeDtypeStruct((b, s, QK_WIDTH), BF16),
                   jax.ShapeDtypeStruct((b, s // tm, ATTN_WIDTH, tm), BF16),
                   jax.ShapeDtypeStruct((b, s, FOURIER_WIDTH), F32)],
        compiler_params=_params(2),
        name="inproj",
    )(x, mod, w_qv_t, w_ku, cos_t, sin_t, cos_tt, sin_tt)


def _attn_kernel(lam_ref, g_ref, qt_ref, k_ref, vt_ref, o_ref, sa_ref, sb_ref, *, lambda_init):
    tq = qt_ref.shape[1]
    n_chunks, _, tk = vt_ref.shape
    row = lax.broadcasted_iota(jnp.int32, (LANES, tq), 0)
    ones = jnp.ones((DENOM_ROWS, tk), BF16)
    lam = (jnp.exp(jnp.sum(lam_ref[0:1, :] * lam_ref[1:2, :], axis=-1, keepdims=True))
           - jnp.exp(jnp.sum(lam_ref[2:3, :] * lam_ref[3:4, :], axis=-1, keepdims=True))
           + lambda_init)

    def queries(hd):
        qt = qt_ref[hd * LANES:(hd + 1) * LANES, :]
        zero = jnp.zeros_like(qt)
        return jnp.concatenate([jnp.where(row < QK_DIM, qt, zero),
                                jnp.where(row >= QK_DIM, qt, zero)], axis=1)

    def scores(hd, qqt, j, s_ref):
        start = pl.multiple_of(j * tk, tk)
        kc = k_ref[pl.ds(start, tk), hd * LANES:(hd + 1) * LANES]
        s = jnp.dot(kc, qqt, preferred_element_type=F32)
        s_ref[...] = s
        return jnp.max(s, axis=0, keepdims=True)

    def accumulate(hd, j, s_ref, mx, state):
        m, acc = state
        m_new = jnp.maximum(m, mx)
        alpha = jnp.exp2(m - m_new)
        p = jnp.exp2(s_ref[...] - m_new).astype(BF16)
        va = jnp.concatenate([vt_ref[j, hd * V_DIM:(hd + 1) * V_DIM, :], ones], axis=0)
        acc = alpha * acc + jnp.dot(va, p, preferred_element_type=F32)
        return m_new, acc

    def finalize(hd, acc):
        on = acc[:V_DIM] / acc[V_DIM:V_DIM + 1]
        ot = on[:, :tq] - lam * on[:, tq:]
        ot = ot * lax.rsqrt(jnp.mean(ot * ot, axis=0, keepdims=True) + LN_EPS)
        o_ref[:, hd * V_DIM:(hd + 1) * V_DIM] = (ot.T * g_ref[...] * (1.0 - lambda_init)).astype(BF16)

    n_trips = (n_chunks - 2) // CHUNKS_PER_TRIP
    qqt = queries(0)
    mx_a = scores(0, qqt, 0, sa_ref)
    for hd in range(N_ATTN_HEADS):
        def pair(j, mx_a, state, hd=hd, qqt=qqt):
            mx_b = scores(hd, qqt, j + 1, sb_ref)
            state = accumulate(hd, j, sa_ref, mx_a, state)
            mx_a = scores(hd, qqt, j + 2, sa_ref)
            state = accumulate(hd, j + 1, sb_ref, mx_b, state)
            return mx_a, state

        def body(i, carry, pair=pair):
            for u in range(0, CHUNKS_PER_TRIP, 2):
                carry = pair(CHUNKS_PER_TRIP * i + u, *carry)
            return carry

        state = (jnp.full((1, 2 * tq), -jnp.inf, F32), jnp.zeros((V_DIM + DENOM_ROWS, 2 * tq), F32))
        carry = lax.fori_loop(0, n_trips, body, (mx_a, state))
        for j in range(n_trips * CHUNKS_PER_TRIP, n_chunks - 2, 2):
            carry = pair(j, *carry)
        mx_a, state = carry
        mx_b = scores(hd, qqt, n_chunks - 1, sb_ref)
        state = accumulate(hd, n_chunks - 2, sa_ref, mx_a, state)
        if hd + 1 < N_ATTN_HEADS:
            qqt = queries(hd + 1)
            mx_a = scores(hd + 1, qqt, 0, sa_ref)
        _, acc = accumulate(hd, n_chunks - 1, sb_ref, mx_b, state)
        finalize(hd, acc)


def _attention(qt, k, vt, lam_vecs, subln_g, lambda_init, tq=512):
    b, s, _ = k.shape
    n_chunks, tk = vt.shape[1], vt.shape[3]
    if (n_chunks - 2) // CHUNKS_PER_TRIP == 0:
        tq *= 2
    tq = min(tq, s)
    assert n_chunks % 2 == 0, "the chunk loop is pipelined over pairs"
    once = pl.Buffered(1)
    return pl.pallas_call(
        functools.partial(_attn_kernel, lambda_init=lambda_init),
        grid=(b, s // tq),
        in_specs=[pl.BlockSpec((4, QK_DIM), lambda bi, i: (0, 0)),
                  pl.BlockSpec((1, V_DIM), lambda bi, i: (0, 0)),
                  pl.BlockSpec((None, QK_WIDTH, tq), lambda bi, i: (bi, 0, i)),
                  pl.BlockSpec((None, s, QK_WIDTH), lambda bi, i: (bi, 0, 0), pipeline_mode=once),
                  pl.BlockSpec((None, n_chunks, ATTN_WIDTH, tk), lambda bi, i: (bi, 0, 0, 0),
                               pipeline_mode=once)],
        out_specs=pl.BlockSpec((None, tq, ATTN_WIDTH), lambda bi, i: (bi, i, 0)),
        out_shape=jax.ShapeDtypeStruct((b, s, ATTN_WIDTH), BF16),
        scratch_shapes=[pltpu.VMEM((tk, 2 * tq), F32), pltpu.VMEM((tk, 2 * tq), F32)],
        compiler_params=_params(2),
        name="attention",
    )(lam_vecs, subln_g, qt, k, vt)


def _dft_tables(n):
    idx = np.arange(n, dtype=np.int64)
    ang = (2.0 * np.pi / n) * ((idx[:, None] * idx[None, :]) % n)
    return np.cos(ang), np.sin(ang)


def _channel_table(seq):
    c, s = _dft_tables(FOURIER_GROUP_DIM)
    scale = 1.0 / math.sqrt(seq * FOURIER_GROUP_DIM)
    return jnp.asarray(np.concatenate([c, -s], axis=0) * scale, F32).astype(BF16)


def _fourier_direct_kernel(u_ref, cs_ref, ch_ref, wf_ref, o_ref):
    u = u_ref[...].astype(BF16)
    pr = jnp.dot(cs_ref[0], u, preferred_element_type=F32)
    qi = jnp.dot(cs_ref[1], u, preferred_element_type=F32)
    ch = ch_ref[...]
    for g in range(N_FOURIER_GROUPS):
        sl = slice(g * LANES, (g + 1) * LANES)
        y = jnp.concatenate([pr[:, sl], qi[:, sl]], axis=1).astype(BF16)
        f = jnp.dot(y, ch, preferred_element_type=F32)
        o_ref[:, sl] = jnp.dot(f.astype(BF16), wf_ref[g].astype(BF16),
                               preferred_element_type=F32).astype(o_ref.dtype)


def _fourier_direct(u, w_f, tt=512):
    b, s, _ = u.shape
    tt = min(tt, s)
    c, sn = _dft_tables(s)
    cs = jnp.asarray(np.stack([c, sn]), F32).astype(BF16)
    return pl.pallas_call(
        _fourier_direct_kernel,
        grid=(s // tt, b),
        in_specs=[pl.BlockSpec((None, s, FOURIER_WIDTH), lambda i, bi: (bi, 0, 0)),
                  pl.BlockSpec((2, tt, s), lambda i, bi: (0, i, 0)),
                  pl.BlockSpec((2 * FOURIER_GROUP_DIM, FOURIER_GROUP_DIM), lambda i, bi: (0, 0)),
                  pl.BlockSpec(w_f.shape, lambda i, bi: (0, 0, 0))],
        out_specs=pl.BlockSpec((None, tt, FOURIER_WIDTH), lambda i, bi: (bi, i, 0)),
        out_shape=jax.ShapeDtypeStruct((b, s, FOURIER_WIDTH), BF16),
        compiler_params=_params(2),
        name="fourier_direct",
    )(u, cs, _channel_table(s), w_f)


def _fourier_two_stage_kernel(u_ref, m1_ref, tw_ref, m2_ref, ch_ref, wf_ref, o_ref,
                              er_ref, em_ref, *, n1, n2, unroll):
    m1 = m1_ref[...]
    cb = tw_ref[0]
    sb = tw_ref[1]
    ab = jnp.dot(ch_ref[...], wf_ref[...].astype(BF16), preferred_element_type=F32)
    a_c = ab[:FOURIER_GROUP_DIM]
    nb_c = ab[FOURIER_GROUP_DIM:]
    mch = jnp.concatenate([jnp.concatenate([a_c, -nb_c], axis=1),
                           jnp.concatenate([nb_c, a_c], axis=1)], axis=0).astype(BF16)

    def stage1(s2, carry):
        cw, sw = carry
        us = u_ref[pl.ds(s2, n1, stride=n2), :].astype(BF16)
        gh = jnp.dot(m1, us, preferred_element_type=F32)
        g = gh[:n1]
        h = gh[n1:]
        gt = jnp.concatenate([g * cw - h * sw, g * sw + h * cw], axis=1).astype(BF16)
        e = jnp.dot(gt, mch, preferred_element_type=F32)
        row0 = pl.multiple_of(s2 * n1, n1)
        er_ref[pl.ds(row0, n1), :] = e[:, :LANES]
        em_ref[pl.ds(row0, n1), :] = e[:, LANES:]
        return cw * cb - sw * sb, sw * cb + cw * sb

    lax.fori_loop(0, n2, stage1, (jnp.ones_like(cb), jnp.zeros_like(sb)), unroll=unroll)

    m2 = m2_ref[...]

    def stage2(t1, carry):
        x = jnp.concatenate([er_ref[pl.ds(t1, n2, stride=n1), :],
                             em_ref[pl.ds(t1, n2, stride=n1), :]], axis=0).astype(BF16)
        o_ref[pl.ds(t1, n2, stride=n1), :] = jnp.dot(m2, x, preferred_element_type=F32)
        return carry

    lax.fori_loop(0, n1, stage2, 0, unroll=unroll)


def _fourier_two_stage(u, w_f, n1=DFT_STAGE1, unroll=4):
    b, s, _ = u.shape
    n2 = s // n1
    c1, s1 = _dft_tables(n1)
    c2, s2 = _dft_tables(n2)
    m1 = jnp.asarray(np.concatenate([c1, s1], axis=0), F32).astype(BF16)
    m2 = jnp.asarray(np.concatenate([c2, -s2], axis=1), F32).astype(BF16)
    ang = (2.0 * np.pi / s) * np.arange(n1, dtype=np.float64)
    tw = np.stack([np.broadcast_to(np.cos(ang)[:, None], (n1, LANES)),
                   np.broadcast_to(np.sin(ang)[:, None], (n1, LANES))])
    tw = jnp.asarray(tw, F32)
    grp = lambda bi, g: (bi, 0, g)
    return pl.pallas_call(
        functools.partial(_fourier_two_stage_kernel, n1=n1, n2=n2, unroll=unroll),
        grid=(b, N_FOURIER_GROUPS),
        in_specs=[pl.BlockSpec((None, s, LANES), grp, pipeline_mode=pl.Buffered(1)),
                  pl.BlockSpec(m1.shape, lambda bi, g: (0, 0)),
                  pl.BlockSpec(tw.shape, lambda bi, g: (0, 0, 0)),
                  pl.BlockSpec(m2.shape, lambda bi, g: (0, 0)),
                  pl.BlockSpec((2 * FOURIER_GROUP_DIM, FOURIER_GROUP_DIM), lambda bi, g: (0, 0)),
                  pl.BlockSpec((None, FOURIER_GROUP_DIM, FOURIER_GROUP_DIM), lambda bi, g: (g, 0, 0))],
        out_specs=pl.BlockSpec((None, s, LANES), grp),
        out_shape=jax.ShapeDtypeStruct((b, s, FOURIER_WIDTH), F32),
        scratch_shapes=[pltpu.VMEM((s, LANES), F32), pltpu.VMEM((s, LANES), F32)],
        compiler_params=_params(2),
        name="fourier_two_stage",
    )(u, m1, tw, m2, _channel_table(s), w_f)


def _fourier(u, w_f):
    if u.shape[1] <= DIRECT_DFT_MAX_SEQ:
        return _fourier_direct(u, w_f)
    return _fourier_two_stage(u, w_f)


def _ffn_kernel(x_ref, o_ref, f_ref, mod_ref, wo_ref, ln1g_ref, ln1b_ref,
                wg_ref, wu_ref, wd_ref, ln2g_ref, ln2b_ref, y_ref, *, n_sub):
    g1 = mod_ref[2:3, :]
    sh2 = mod_ref[3:4, :]
    sc2 = mod_ref[4:5, :]
    g2 = mod_ref[5:6, :]
    rows = x_ref.shape[0] // n_sub
    subs = [slice(r * rows, (r + 1) * rows) for r in range(n_sub)]
    hs, ffns = [], []
    for sl in subs:
        mix = (jnp.dot(o_ref[sl, :], wo_ref[:ATTN_WIDTH, :], preferred_element_type=F32)
               + jnp.dot(f_ref[sl, :].astype(BF16), wo_ref[ATTN_WIDTH:, :], preferred_element_type=F32))
        x1 = _layer_norm(ALPHA * x_ref[sl, :] + (1.0 + g1) * mix) * ln1g_ref[...] + ln1b_ref[...]
        y_ref[sl, :] = x1
        hs.append((_layer_norm(x1) * (1.0 + sc2) + sh2).astype(BF16))
    for h in hs:
        gate = jnp.dot(h, wg_ref[...], preferred_element_type=F32)
        up = jnp.dot(h, wu_ref[...], preferred_element_type=F32)
        a = (_silu(gate) * up).astype(BF16)
        ffns.append(jnp.dot(a, wd_ref[...], preferred_element_type=F32))
    for sl, ffn in zip(subs, ffns):
        y_ref[sl, :] = _layer_norm(ALPHA * y_ref[sl, :] + (1.0 + g2) * ffn) * ln2g_ref[...] + ln2b_ref[...]


def _ffn(x, o, f, mod, wo, ln1g, ln1b, wg, wu, wd, ln2g, ln2b, tm=1024, n_sub=4):
    b, s, d = x.shape
    tm = min(tm, s)
    tok = lambda bi, i: (bi, i, 0)
    const = lambda bi, i: (0, 0)
    resident = lambda shape: pl.BlockSpec(shape, const, pipeline_mode=pl.Buffered(1))
    return pl.pallas_call(
        functools.partial(_ffn_kernel, n_sub=n_sub),
        grid=(b, s // tm),
        in_specs=[pl.BlockSpec((None, tm, d), tok),
                  pl.BlockSpec((None, tm, ATTN_WIDTH), tok),
                  pl.BlockSpec((None, tm, FOURIER_WIDTH), tok),
                  pl.BlockSpec((None, N_MOD, d), lambda bi, i: (bi, 0, 0)),
                  resident(wo.shape), resident((1, d)), resident((1, d)),
                  resident(wg.shape), resident(wu.shape), resident(wd.shape),
                  resident((1, d)), resident((1, d))],
        out_specs=pl.BlockSpec((None, tm, d), tok),
        out_shape=jax.ShapeDtypeStruct((b, s, d), F32),
        compiler_params=_params(2),
        name="ffn",
    )(x, o, f, mod, wo, ln1g, ln1b, wg, wu, wd, ln2g, ln2b)


def _rope_tables(seq):
    inv = ROPE_THETA ** (-jnp.arange(0, QK_DIM, 2, dtype=F32) / QK_DIM)
    ang = jnp.arange(seq, dtype=F32)[:, None] * inv[None, :]
    cos = jnp.cos(ang)
    sin = jnp.sin(ang)
    reps = LANES // (QK_DIM // 2)
    sign = jnp.tile(jnp.concatenate([-jnp.ones((QK_DIM // 2,), F32), jnp.ones((QK_DIM // 2,), F32)]),
                    LANES // QK_DIM)
    return jnp.tile(cos, (1, reps)), jnp.tile(sin, (1, reps)) * sign, cos.T, sin.T


def _layer(x, mod, l, w_qv_t, w_ku, lam_vecs, subln_g, w_fourier, w_out, ln1_g, ln1_b,
           w_gate, w_up, w_down, ln2_g, ln2_b):
    lambda_init = 0.8 - 0.6 * math.exp(-0.3 * l)
    s = x.shape[1]
    qt, k, vt, u = _inproj(x, mod, w_qv_t, w_ku, _rope_tables(s), tm=min(KV_CHUNK, s))
    o = _attention(qt, k, vt, lam_vecs, subln_g, lambda_init)
    f = _fourier(u, w_fourier)
    return _ffn(x, o, f, mod, w_out, ln1_g, ln1_b, w_gate, w_up, w_down, ln2_g, ln2_b)


def _split_w_in(w_in):
    q, k, v, u = jnp.split(w_in, [QK_WIDTH, 2 * QK_WIDTH, 2 * QK_WIDTH + ATTN_WIDTH], axis=1)
    return (jnp.concatenate([q, v], axis=1).T.astype(BF16), jnp.concatenate([k, u], axis=1).astype(BF16))


def kernel(x_prompt, x_sample, c_prompt, c_sample, w_ada, b_ada, w_in, lambda_q1, lambda_k1, lambda_q2, lambda_k2, subln_g, w_fourier, w_out, ln1_g, ln1_b, w_gate, w_up, w_down, ln2_g, ln2_b):
    n_prompt = c_prompt.shape[0]
    c_all = jnp.concatenate([c_prompt, c_sample], axis=0)
    y_prompt, y_sample = x_prompt, x_sample
    for l in range(DEPTH):
        mod = _ada(c_all, w_ada[l], b_ada[l][None, :]).reshape(c_all.shape[0], N_MOD, D_MODEL)
        lam_vecs = jnp.stack([lambda_q1[l], lambda_k1[l], lambda_q2[l], lambda_k2[l]]).astype(F32)
        weights = (*_split_w_in(w_in[l]), lam_vecs, subln_g[l][None, :], w_fourier[l],
                   w_out[l].astype(BF16), ln1_g[l][None, :], ln1_b[l][None, :],
                   w_gate[l].astype(BF16), w_up[l].astype(BF16), w_down[l].astype(BF16),
                   ln2_g[l][None, :], ln2_b[l][None, :])
        y_prompt = _layer(y_prompt, mod[:n_prompt], l, *weights)
        y_sample = _layer(y_sample, mod[n_prompt:], l, *weights)
    return (y_prompt, y_sample)
```

```python
import functools
import math

import numpy as np
import jax
import jax.numpy as jnp
from jax import lax
from jax.experimental import pallas as pl
from jax.experimental.pallas import tpu as pltpu

D_MODEL = 1024
DEPTH = 1
N_ATTN_HEADS = 4
QK_DIM = 64
V_DIM = 2 * QK_DIM
ATTN_WIDTH = N_ATTN_HEADS * V_DIM
QK_WIDTH = N_ATTN_HEADS * 2 * QK_DIM
N_FOURIER_GROUPS = 4
FOURIER_GROUP_DIM = 128
FOURIER_WIDTH = N_FOURIER_GROUPS * FOURIER_GROUP_DIM
IN_WIDTH = 2 * QK_WIDTH + ATTN_WIDTH + FOURIER_WIDTH
D_FF = -(-8 * D_MODEL // (3 * 256)) * 256
ROPE_THETA = 10000.0
LN_EPS = 1e-5
ALPHA = (2 * DEPTH) ** 0.25
N_MOD = 6

LANES = 128
VMEM_LIMIT = 56 * 1024 * 1024
DIRECT_DFT_MAX_SEQ = 2048
DFT_STAGE1 = 128
KV_CHUNK = 512
CHUNKS_PER_TRIP = 10
DENOM_ROWS = 16

F32 = jnp.float32
BF16 = jnp.bfloat16


def _params(n_grid_dims):
    return pltpu.CompilerParams(
        dimension_semantics=("arbitrary",) * n_grid_dims, vmem_limit_bytes=VMEM_LIMIT)


def _layer_norm(x):
    mu = jnp.mean(x, axis=-1, keepdims=True)
    xc = x - mu
    var = jnp.mean(xc * xc, axis=-1, keepdims=True)
    return xc * lax.rsqrt(var + LN_EPS)


def _silu(x):
    return x * (1.0 / (1.0 + jnp.exp(-x)))


def _ada_kernel(c_ref, w_ref, b_ref, o_ref):
    s = _silu(c_ref[...])
    o_ref[...] = jnp.dot(s, w_ref[...], preferred_element_type=F32,
                         precision=lax.Precision.HIGHEST) + b_ref[...]


def _ada(c, w, b, tn=1536):
    n_rows, d = c.shape
    n_out = w.shape[1]
    return pl.pallas_call(
        _ada_kernel,
        grid=(n_out // tn,),
        in_specs=[pl.BlockSpec((n_rows, d), lambda j: (0, 0)),
                  pl.BlockSpec((d, tn), lambda j: (0, j)),
                  pl.BlockSpec((1, tn), lambda j: (0, j))],
        out_specs=pl.BlockSpec((n_rows, tn), lambda j: (0, j)),
        out_shape=jax.ShapeDtypeStruct((n_rows, n_out), F32),
        compiler_params=_params(1),
        name="ada",
    )(c, w, b)


def _inproj_kernel(x_ref, mod_ref, wqv_ref, wku_ref, cos_ref, sin_ref, cost_ref, sint_ref,
                   qt_ref, k_ref, vt_ref, u_ref):
    h = (_layer_norm(x_ref[...]) * (1.0 + mod_ref[1:2, :]) + mod_ref[0:1, :]).astype(BF16)
    qvt = lax.dot_general(wqv_ref[...], h, (((1,), (1,)), ((), ())), preferred_element_type=F32)
    cost = cost_ref[...]
    sint = sint_ref[...]
    half = QK_DIM // 2
    scale = QK_DIM ** -0.5 * math.log2(math.e)
    for c in range(QK_WIDTH // QK_DIM):
        a = qvt[c * QK_DIM: c * QK_DIM + half, :]
        b = qvt[c * QK_DIM + half: (c + 1) * QK_DIM, :]
        qt_ref[c * QK_DIM: c * QK_DIM + half, :] = ((a * cost - b * sint) * scale).astype(BF16)
        qt_ref[c * QK_DIM + half: (c + 1) * QK_DIM, :] = ((b * cost + a * sint) * scale).astype(BF16)
    vt_ref[...] = qvt[QK_WIDTH:, :].astype(BF16)

    ku = jnp.dot(h, wku_ref[...], preferred_element_type=F32)
    cos = cos_ref[...]
    sin = sin_ref[...]
    lane = lax.broadcasted_iota(jnp.int32, cos.shape, 1)
    first_half = (lane % QK_DIM) < half
    for hd in range(N_ATTN_HEADS):
        xh = ku[:, hd * LANES:(hd + 1) * LANES]
        partner = jnp.where(first_half, pltpu.roll(xh, LANES - half, 1), pltpu.roll(xh, half, 1))
        k_ref[:, hd * LANES:(hd + 1) * LANES] = (xh * cos + partner * sin).astype(BF16)
    u_ref[...] = ku[:, QK_WIDTH:]


def _inproj(x, mod, w_qv_t, w_ku, rope, tm):
    b, s, d = x.shape
    cos_t, sin_t, cos_tt, sin_tt = rope
    tok = lambda bi, i: (bi, i, 0)
    const = lambda bi, i: (0, 0)
    return pl.pallas_call(
        _inproj_kernel,
        grid=(b, s // tm),
        in_specs=[pl.BlockSpec((None, tm, d), tok),
                  pl.BlockSpec((None, N_MOD, d), lambda bi, i: (bi, 0, 0)),
                  pl.BlockSpec(w_qv_t.shape, const),
                  pl.BlockSpec(w_ku.shape, const),
                  pl.BlockSpec((tm, LANES), lambda bi, i: (i, 0)),
                  pl.BlockSpec((tm, LANES), lambda bi, i: (i, 0)),
                  pl.BlockSpec((QK_DIM // 2, tm), lambda bi, i: (0, i)),
                  pl.BlockSpec((QK_DIM // 2, tm), lambda bi, i: (0, i))],
        out_specs=[pl.BlockSpec((None, QK_WIDTH, tm), lambda bi, i: (bi, 0, i)),
                   pl.BlockSpec((None, tm, QK_WIDTH), tok),
                   pl.BlockSpec((None, None, ATTN_WIDTH, tm), lambda bi, i: (bi, i, 0, 0)),
                   pl.BlockSpec((None, tm, FOURIER_WIDTH), tok)],
        out_shape=[jax.ShapeDtypeStruct((b, QK_WIDTH, s), BF16),
                   jax.ShapeDtypeStruct((b, s, QK_WIDTH), BF16),
                   jax.ShapeDtypeStruct((b, s // tm, ATTN_WIDTH, tm), BF16),
                   jax.ShapeDtypeStruct((b, s, FOURIER_WIDTH), F32)],
        compiler_params=_params(2),
        name="inproj",
    )(x, mod, w_qv_t, w_ku, cos_t, sin_t, cos_tt, sin_tt)


def _attn_kernel(lam_ref, g_ref, qt_ref, k_ref, vt_ref, o_ref, sa_ref, sb_ref, *, lambda_init):
    tq = qt_ref.shape[1]
    n_chunks, _, tk = vt_ref.shape
    row = lax.broadcasted_iota(jnp.int32, (LANES, tq), 0)
    ones = jnp.ones((DENOM_ROWS, tk), BF16)
    lam = (jnp.exp(jnp.sum(lam_ref[0:1, :] * lam_ref[1:2, :], axis=-1, keepdims=True))
           - jnp.exp(jnp.sum(lam_ref[2:3, :] * lam_ref[3:4, :], axis=-1, keepdims=True))
           + lambda_init)

    def queries(hd):
        qt = qt_ref[hd * LANES:(hd + 1) * LANES, :]
        zero = jnp.zeros_like(qt)
        return jnp.concatenate([jnp.where(row < QK_DIM, qt, zero),
                                jnp.where(row >= QK_DIM, qt, zero)], axis=1)

    def scores(hd, qqt, j, s_ref):
        start = pl.multiple_of(j * tk, tk)
        kc = k_ref[pl.ds(start, tk), hd * LANES:(hd + 1) * LANES]
        s = jnp.dot(kc, qqt, preferred_element_type=F32)
        s_ref[...] = s
        return jnp.max(s, axis=0, keepdims=True)

    def accumulate(hd, j, s_ref, mx, state):
        m, acc = state
        m_new = jnp.maximum(m, mx)
        alpha = jnp.exp2(m - m_new)
        p = jnp.exp2(s_ref[...] - m_new).astype(BF16)
        va = jnp.concatenate([vt_ref[j, hd * V_DIM:(hd + 1) * V_DIM, :], ones], axis=0)
        acc = alpha * acc + jnp.dot(va, p, preferred_element_type=F32)
        return m_new, acc

    def finalize(hd, acc):
        on = acc[:V_DIM] / acc[V_DIM:V_DIM + 1]
        ot = on[:, :tq] - lam * on[:, tq:]
        ot = ot * lax.rsqrt(jnp.mean(ot * ot, axis=0, keepdims=True) + LN_EPS)
        o_ref[:, hd * V_DIM:(hd + 1) * V_DIM] = (ot.T * g_ref[...] * (1.0 - lambda_init)).astype(BF16)

    n_trips = (n_chunks - 2) // CHUNKS_PER_TRIP
    qqt = queries(0)
    mx_a = scores(0, qqt, 0, sa_ref)
    for hd in range(N_ATTN_HEADS):
        def pair(j, mx_a, state, hd=hd, qqt=qqt):
            mx_b = scores(hd, qqt, j + 1, sb_ref)
            state = accumulate(hd, j, sa_ref, mx_a, state)
            mx_a = scores(hd, qqt, j + 2, sa_ref)
            state = accumulate(hd, j + 1, sb_ref, mx_b, state)
            return mx_a, state

        def body(i, carry, pair=pair):
            for u in range(0, CHUNKS_PER_TRIP, 2):
                carry = pair(CHUNKS_PER_TRIP * i + u, *carry)
            return carry

        state = (jnp.full((1, 2 * tq), -jnp.inf, F32), jnp.zeros((V_DIM + DENOM_ROWS, 2 * tq), F32))
        carry = lax.fori_loop(0, n_trips, body, (mx_a, state))
        for j in range(n_trips * CHUNKS_PER_TRIP, n_chunks - 2, 2):
            carry = pair(j, *carry)
        mx_a, state = carry
        mx_b = scores(hd, qqt, n_chunks - 1, sb_ref)
        state = accumulate(hd, n_chunks - 2, sa_ref, mx_a, state)
        if hd + 1 < N_ATTN_HEADS:
            qqt = queries(hd + 1)
            mx_a = scores(hd + 1, qqt, 0, sa_ref)
        _, acc = accumulate(hd, n_chunks - 1, sb_ref, mx_b, state)
        finalize(hd, acc)


def _attention(qt, k, vt, lam_vecs, subln_g, lambda_init, tq=512):
    b, s, _ = k.shape
    n_chunks, tk = vt.shape[1], vt.shape[3]
    if (n_chunks - 2) // CHUNKS_PER_TRIP == 0:
        tq *= 2
    tq = min(tq, s)
    assert n_chunks % 2 == 0, "the chunk loop is pipelined over pairs"
    once = pl.Buffered(1)
    return pl.pallas_call(
        functools.partial(_attn_kernel, lambda_init=lambda_init),
        grid=(b, s // tq),
        in_specs=[pl.BlockSpec((4, QK_DIM), lambda bi, i: (0, 0)),
                  pl.BlockSpec((1, V_DIM), lambda bi, i: (0, 0)),
                  pl.BlockSpec((None, QK_WIDTH, tq), lambda bi, i: (bi, 0, i)),
                  pl.BlockSpec((None, s, QK_WIDTH), lambda bi, i: (bi, 0, 0), pipeline_mode=once),
                  pl.BlockSpec((None, n_chunks, ATTN_WIDTH, tk), lambda bi, i: (bi, 0, 0, 0),
                               pipeline_mode=once)],
        out_specs=pl.BlockSpec((None, tq, ATTN_WIDTH), lambda bi, i: (bi, i, 0)),
        out_shape=jax.ShapeDtypeStruct((b, s, ATTN_WIDTH), BF16),
        scratch_shapes=[pltpu.VMEM((tk, 2 * tq), F32), pltpu.VMEM((tk, 2 * tq), F32)],
        compiler_params=_params(2),
        name="attention",
    )(lam_vecs, subln_g, qt, k, vt)


def _dft_tables(n):
    idx = np.arange(n, dtype=np.int64)
    ang = (2.0 * np.pi / n) * ((idx[:, None] * idx[None, :]) % n)
    return np.cos(ang), np.sin(ang)


def _channel_table(seq):
    c, s = _dft_tables(FOURIER_GROUP_DIM)
    scale = 1.0 / math.sqrt(seq * FOURIER_GROUP_DIM)
    return jnp.asarray(np.concatenate([c, -s], axis=0) * scale, F32).astype(BF16)


def _fourier_direct_kernel(u_ref, cs_ref, ch_ref, wf_ref, o_ref):
    u = u_ref[...].astype(BF16)
    pr = jnp.dot(cs_ref[0], u, preferred_element_type=F32)
    qi = jnp.dot(cs_ref[1], u, preferred_element_type=F32)
    ch = ch_ref[...]
    for g in range(N_FOURIER_GROUPS):
        sl = slice(g * LANES, (g + 1) * LANES)
        y = jnp.concatenate([pr[:, sl], qi[:, sl]], axis=1).astype(BF16)
        f = jnp.dot(y, ch, preferred_element_type=F32)
        o_ref[:, sl] = jnp.dot(f.astype(BF16), wf_ref[g].astype(BF16),
                               preferred_element_type=F32).astype(o_ref.dtype)


def _fourier_direct(u, w_f, tt=512):
    b, s, _ = u.shape
    tt = min(tt, s)
    c, sn = _dft_tables(s)
    cs = jnp.asarray(np.stack([c, sn]), F32).astype(BF16)
    return pl.pallas_call(
        _fourier_direct_kernel,
        grid=(s // tt, b),
        in_specs=[pl.BlockSpec((None, s, FOURIER_WIDTH), lambda i, bi: (bi, 0, 0)),
                  pl.BlockSpec((2, tt, s), lambda i, bi: (0, i, 0)),
                  pl.BlockSpec((2 * FOURIER_GROUP_DIM, FOURIER_GROUP_DIM), lambda i, bi: (0, 0)),
                  pl.BlockSpec(w_f.shape, lambda i, bi: (0, 0, 0))],
        out_specs=pl.BlockSpec((None, tt, FOURIER_WIDTH), lambda i, bi: (bi, i, 0)),
        out_shape=jax.ShapeDtypeStruct((b, s, FOURIER_WIDTH), BF16),
        compiler_params=_params(2),
        name="fourier_direct",
    )(u, cs, _channel_table(s), w_f)


def _fourier_two_stage_kernel(u_ref, m1_ref, tw_ref, m2_ref, ch_ref, wf_ref, o_ref,
                              er_ref, em_ref, *, n1, n2, unroll1, unroll2):
    m1 = m1_ref[...]
    cb = tw_ref[0]
    sb = tw_ref[1]
    ab = jnp.dot(ch_ref[...], wf_ref[...].astype(BF16), preferred_element_type=F32)
    a_c = ab[:FOURIER_GROUP_DIM]
    nb_c = ab[FOURIER_GROUP_DIM:]
    mch = jnp.concatenate([jnp.concatenate([a_c, -nb_c], axis=1),
                           jnp.concatenate([nb_c, a_c], axis=1)], axis=0).astype(BF16)

    def stage1(s2, carry):
        cw, sw = carry
        us = u_ref[pl.ds(s2, n1, stride=n2), :].astype(BF16)
        gh = jnp.dot(m1, us, preferred_element_type=F32)
        g = gh[:n1]
        h = gh[n1:]
        gt = jnp.concatenate([g * cw - h * sw, g * sw + h * cw], axis=1).astype(BF16)
        e = jnp.dot(gt, mch, preferred_element_type=F32)
        row0 = pl.multiple_of(s2 * n1, n1)
        er_ref[pl.ds(row0, n1), :] = e[:, :LANES]
        em_ref[pl.ds(row0, n1), :] = e[:, LANES:]
        return cw * cb - sw * sb, sw * cb + cw * sb

    lax.fori_loop(0, n2, stage1, (jnp.ones_like(cb), jnp.zeros_like(sb)), unroll=unroll1)

    m2 = m2_ref[...]

    def stage2(t1, carry):
        x = jnp.concatenate([er_ref[pl.ds(t1, n2, stride=n1), :],
                             em_ref[pl.ds(t1, n2, stride=n1), :]], axis=0).astype(BF16)
        o_ref[pl.ds(t1, n2, stride=n1), :] = jnp.dot(m2, x, preferred_element_type=F32)
        return carry

    lax.fori_loop(0, n1, stage2, 0, unroll=unroll2)


def _fourier_two_stage(u, w_f, n1=DFT_STAGE1, unroll1=16, unroll2=8):
    b, s, _ = u.shape
    n2 = s // n1
    c1, s1 = _dft_tables(n1)
    c2, s2 = _dft_tables(n2)
    m1 = jnp.asarray(np.concatenate([c1, s1], axis=0), F32).astype(BF16)
    m2 = jnp.asarray(np.concatenate([c2, -s2], axis=1), F32).astype(BF16)
    ang = (2.0 * np.pi / s) * np.arange(n1, dtype=np.float64)
    tw = np.stack([np.broadcast_to(np.cos(ang)[:, None], (n1, LANES)),
                   np.broadcast_to(np.sin(ang)[:, None], (n1, LANES))])
    tw = jnp.asarray(tw, F32)
    grp = lambda bi, g: (bi, 0, g)
    return pl.pallas_call(
        functools.partial(_fourier_two_stage_kernel, n1=n1, n2=n2,
                          unroll1=min(unroll1, n2), unroll2=min(unroll2, n1)),
        grid=(b, N_FOURIER_GROUPS),
        in_specs=[pl.BlockSpec((None, s, LANES), grp, pipeline_mode=pl.Buffered(1)),
                  pl.BlockSpec(m1.shape, lambda bi, g: (0, 0)),
                  pl.BlockSpec(tw.shape, lambda bi, g: (0, 0, 0)),
                  pl.BlockSpec(m2.shape, lambda bi, g: (0, 0)),
                  pl.BlockSpec((2 * FOURIER_GROUP_DIM, FOURIER_GROUP_DIM), lambda bi, g: (0, 0)),
                  pl.BlockSpec((None, FOURIER_GROUP_DIM, FOURIER_GROUP_DIM), lambda bi, g: (g, 0, 0))],
        out_specs=pl.BlockSpec((None, s, LANES), grp),
        out_shape=jax.ShapeDtypeStruct((b, s, FOURIER_WIDTH), F32),
        scratch_shapes=[pltpu.VMEM((s, LANES), F32), pltpu.VMEM((s, LANES), F32)],
        compiler_params=_params(2),
        name="fourier_two_stage",
    )(u, m1, tw, m2, _channel_table(s), w_f)


def _fourier(u, w_f):
    if u.shape[1] <= DIRECT_DFT_MAX_SEQ:
        return _fourier_direct(u, w_f)
    return _fourier_two_stage(u, w_f)


def _ffn_kernel(x_ref, o_ref, f_ref, mod_ref, wo_ref, ln1g_ref, ln1b_ref,
                wg_ref, wu_ref, wd_ref, ln2g_ref, ln2b_ref, y_ref, *, n_sub):
    g1 = mod_ref[2:3, :]
    sh2 = mod_ref[3:4, :]
    sc2 = mod_ref[4:5, :]
    g2 = mod_ref[5:6, :]
    rows = x_ref.shape[0] // n_sub
    subs = [slice(r * rows, (r + 1) * rows) for r in range(n_sub)]
    hs, ffns = [], []
    for sl in subs:
        mix = (jnp.dot(o_ref[sl, :], wo_ref[:ATTN_WIDTH, :], preferred_element_type=F32)
               + jnp.dot(f_ref[sl, :].astype(BF16), wo_ref[ATTN_WIDTH:, :], preferred_element_type=F32))
        x1 = _layer_norm(ALPHA * x_ref[sl, :] + (1.0 + g1) * mix) * ln1g_ref[...] + ln1b_ref[...]
        y_ref[sl, :] = x1
        hs.append((_layer_norm(x1) * (1.0 + sc2) + sh2).astype(BF16))
    for h in hs:
        gate = jnp.dot(h, wg_ref[...], preferred_element_type=F32)
        up = jnp.dot(h, wu_ref[...], preferred_element_type=F32)
        a = (_silu(gate) * up).astype(BF16)
        ffns.append(jnp.dot(a, wd_ref[...], preferred_element_type=F32))
    for sl, ffn in zip(subs, ffns):
        y_ref[sl, :] = _layer_norm(ALPHA * y_ref[sl, :] + (1.0 + g2) * ffn) * ln2g_ref[...] + ln2b_ref[...]


def _ffn(x, o, f, mod, wo, ln1g, ln1b, wg, wu, wd, ln2g, ln2b, tm=1024, n_sub=4):
    b, s, d = x.shape
    tm = min(tm, s)
    tok = lambda bi, i: (bi, i, 0)
    const = lambda bi, i: (0, 0)
    resident = lambda shape: pl.BlockSpec(shape, const, pipeline_mode=pl.Buffered(1))
    return pl.pallas_call(
        functools.partial(_ffn_kernel, n_sub=n_sub),
        grid=(b, s // tm),
        in_specs=[pl.BlockSpec((None, tm, d), tok),
                  pl.BlockSpec((None, tm, ATTN_WIDTH), tok),
                  pl.BlockSpec((None, tm, FOURIER_WIDTH), tok),
                  pl.BlockSpec((None, N_MOD, d), lambda bi, i: (bi, 0, 0)),
                  resident(wo.shape), resident((1, d)), resident((1, d)),
                  resident(wg.shape), resident(wu.shape), resident(wd.shape),
                  resident((1, d)), resident((1, d))],
        out_specs=pl.BlockSpec((None, tm, d), tok),
        out_shape=jax.ShapeDtypeStruct((b, s, d), F32),
        compiler_params=_params(2),
        name="ffn",
    )(x, o, f, mod, wo, ln1g, ln1b, wg, wu, wd, ln2g, ln2b)


def _rope_tables(seq):
    inv = ROPE_THETA ** (-jnp.arange(0, QK_DIM, 2, dtype=F32) / QK_DIM)
    ang = jnp.arange(seq, dtype=F32)[:, None] * inv[None, :]
    cos = jnp.cos(ang)
    sin = jnp.sin(ang)
    reps = LANES // (QK_DIM // 2)
    sign = jnp.tile(jnp.concatenate([-jnp.ones((QK_DIM // 2,), F32), jnp.ones((QK_DIM // 2,), F32)]),
                    LANES // QK_DIM)
    return jnp.tile(cos, (1, reps)), jnp.tile(sin, (1, reps)) * sign, cos.T, sin.T


def _layer(x, mod, l, w_qv_t, w_ku, lam_vecs, subln_g, w_fourier, w_out, ln1_g, ln1_b,
           w_gate, w_up, w_down, ln2_g, ln2_b):
    lambda_init = 0.8 - 0.6 * math.exp(-0.3 * l)
    s = x.shape[1]
    qt, k, vt, u = _inproj(x, mod, w_qv_t, w_ku, _rope_tables(s), tm=min(KV_CHUNK, s))
    o = _attention(qt, k, vt, lam_vecs, subln_g, lambda_init)
    f = _fourier(u, w_fourier)
    return _ffn(x, o, f, mod, w_out, ln1_g, ln1_b, w_gate, w_up, w_down, ln2_g, ln2_b)


def _split_w_in(w_in):
    q, k, v, u = jnp.split(w_in, [QK_WIDTH, 2 * QK_WIDTH, 2 * QK_WIDTH + ATTN_WIDTH], axis=1)
    return (jnp.concatenate([q, v], axis=1).T.astype(BF16), jnp.concatenate([k, u], axis=1).astype(BF16))


def kernel(x_prompt, x_sample, c_prompt, c_sample, w_ada, b_ada, w_in, lambda_q1, lambda_k1, lambda_q2, lambda_k2, subln_g, w_fourier, w_out, ln1_g, ln1_b, w_gate, w_up, w_down, ln2_g, ln2_b):
    n_prompt = c_prompt.shape[0]
    c_all = jnp.concatenate([c_prompt, c_sample], axis=0)
    y_prompt, y_sample = x_prompt, x_sample
    for l in range(DEPTH):
        mod = _ada(c_all, w_ada[l], b_ada[l][None, :]).reshape(c_all.shape[0], N_MOD, D_MODEL)
        lam_vecs = jnp.stack([lambda_q1[l], lambda_k1[l], lambda_q2[l], lambda_k2[l]]).astype(F32)
        weights = (*_split_w_in(w_in[l]), lam_vecs, subln_g[l][None, :], w_fourier[l],
                   w_out[l].astype(BF16), ln1_g[l][None, :], ln1_b[l][None, :],
                   w_gate[l].astype(BF16), w_up[l].astype(BF16), w_down[l].astype(BF16),
                   ln2_g[l][None, :], ln2_b[l][None, :])
        y_prompt = _layer(y_prompt, mod[:n_prompt], l, *weights)
        y_sample = _layer(y_sample, mod[n_prompt:], l, *weights)
    return (y_prompt, y_sample)
```

```python
import functools
import math

import numpy as np
import jax
import jax.numpy as jnp
from jax import lax
from jax.experimental import pallas as pl
from jax.experimental.pallas import tpu as pltpu

D_MODEL = 1024
DEPTH = 1
N_ATTN_HEADS = 4
QK_DIM = 64
V_DIM = 2 * QK_DIM
ATTN_WIDTH = N_ATTN_HEADS * V_DIM
QK_WIDTH = N_ATTN_HEADS * 2 * QK_DIM
N_FOURIER_GROUPS = 4
FOURIER_GROUP_DIM = 128
FOURIER_WIDTH = N_FOURIER_GROUPS * FOURIER_GROUP_DIM
IN_WIDTH = 2 * QK_WIDTH + ATTN_WIDTH + FOURIER_WIDTH
D_FF = -(-8 * D_MODEL // (3 * 256)) * 256
ROPE_THETA = 10000.0
LN_EPS = 1e-5
ALPHA = (2 * DEPTH) ** 0.25
N_MOD = 6

LANES = 128
VMEM_LIMIT = 56 * 1024 * 1024
DIRECT_DFT_MAX_SEQ = 2048
DFT_STAGE1 = 128
KV_CHUNK = 512
CHUNKS_PER_TRIP = 10
DENOM_ROWS = 16

F32 = jnp.float32
BF16 = jnp.bfloat16


def _params(n_grid_dims):
    return pltpu.CompilerParams(
        dimension_semantics=("arbitrary",) * n_grid_dims, vmem_limit_bytes=VMEM_LIMIT)


def _layer_norm(x):
    mu = jnp.mean(x, axis=-1, keepdims=True)
    xc = x - mu
    var = jnp.mean(xc * xc, axis=-1, keepdims=True)
    return xc * lax.rsqrt(var + LN_EPS)


def _silu(x):
    return x * (1.0 / (1.0 + jnp.exp(-x)))


def _ada_kernel(c_ref, w_ref, b_ref, o_ref):
    s = _silu(c_ref[...])
    o_ref[...] = jnp.dot(s.astype(BF16), w_ref[...].astype(BF16),
                         preferred_element_type=F32) + b_ref[...]


def _ada(c, w, b, tn=1536):
    n_rows, d = c.shape
    n_out = w.shape[1]
    return pl.pallas_call(
        _ada_kernel,
        grid=(n_out // tn,),
        in_specs=[pl.BlockSpec((n_rows, d), lambda j: (0, 0)),
                  pl.BlockSpec((d, tn), lambda j: (0, j)),
                  pl.BlockSpec((1, tn), lambda j: (0, j))],
        out_specs=pl.BlockSpec((n_rows, tn), lambda j: (0, j)),
        out_shape=jax.ShapeDtypeStruct((n_rows, n_out), F32),
        compiler_params=_params(1),
        name="ada",
    )(c, w, b)


def _inproj_kernel(x_ref, mod_ref, wqv_ref, wku_ref, cos_ref, sin_ref, cost_ref, sint_ref,
                   qt_ref, k_ref, vt_ref, u_ref):
    h = (_layer_norm(x_ref[...]) * (1.0 + mod_ref[1:2, :]) + mod_ref[0:1, :]).astype(BF16)
    qvt = lax.dot_general(wqv_ref[...], h, (((1,), (1,)), ((), ())), preferred_element_type=F32)
    cost = cost_ref[...]
    sint = sint_ref[...]
    half = QK_DIM // 2
    scale = QK_DIM ** -0.5 * math.log2(math.e)
    for c in range(QK_WIDTH // QK_DIM):
        a = qvt[c * QK_DIM: c * QK_DIM + half, :]
        b = qvt[c * QK_DIM + half: (c + 1) * QK_DIM, :]
        qt_ref[c * QK_DIM: c * QK_DIM + half, :] = ((a * cost - b * sint) * scale).astype(BF16)
        qt_ref[c * QK_DIM + half: (c + 1) * QK_DIM, :] = ((b * cost + a * sint) * scale).astype(BF16)
    vt_ref[...] = qvt[QK_WIDTH:, :].astype(BF16)

    ku = jnp.dot(h, wku_ref[...], preferred_element_type=F32)
    cos = cos_ref[...]
    sin = sin_ref[...]
    lane = lax.broadcasted_iota(jnp.int32, cos.shape, 1)
    first_half = (lane % QK_DIM) < half
    for hd in range(N_ATTN_HEADS):
        xh = ku[:, hd * LANES:(hd + 1) * LANES]
        partner = jnp.where(first_half, pltpu.roll(xh, LANES - half, 1), pltpu.roll(xh, half, 1))
        k_ref[:, hd * LANES:(hd + 1) * LANES] = (xh * cos + partner * sin).astype(BF16)
    u_ref[...] = ku[:, QK_WIDTH:]


def _inproj(x, mod, w_qv_t, w_ku, rope, tm):
    b, s, d = x.shape
    cos_t, sin_t, cos_tt, sin_tt = rope
    tok = lambda bi, i: (bi, i, 0)
    const = lambda bi, i: (0, 0)
    return pl.pallas_call(
        _inproj_kernel,
        grid=(b, s // tm),
        in_specs=[pl.BlockSpec((None, tm, d), tok),
                  pl.BlockSpec((None, N_MOD, d), lambda bi, i: (bi, 0, 0)),
                  pl.BlockSpec(w_qv_t.shape, const),
                  pl.BlockSpec(w_ku.shape, const),
                  pl.BlockSpec((tm, LANES), lambda bi, i: (i, 0)),
                  pl.BlockSpec((tm, LANES), lambda bi, i: (i, 0)),
                  pl.BlockSpec((QK_DIM // 2, tm), lambda bi, i: (0, i)),
                  pl.BlockSpec((QK_DIM // 2, tm), lambda bi, i: (0, i))],
        out_specs=[pl.BlockSpec((None, QK_WIDTH, tm), lambda bi, i: (bi, 0, i)),
                   pl.BlockSpec((None, tm, QK_WIDTH), tok),
                   pl.BlockSpec((None, None, ATTN_WIDTH, tm), lambda bi, i: (bi, i, 0, 0)),
                   pl.BlockSpec((None, tm, FOURIER_WIDTH), tok)],
        out_shape=[jax.ShapeDtypeStruct((b, QK_WIDTH, s), BF16),
                   jax.ShapeDtypeStruct((b, s, QK_WIDTH), BF16),
                   jax.ShapeDtypeStruct((b, s // tm, ATTN_WIDTH, tm), BF16),
                   jax.ShapeDtypeStruct((b, s, FOURIER_WIDTH), F32)],
        compiler_params=_params(2),
        name="inproj",
    )(x, mod, w_qv_t, w_ku, cos_t, sin_t, cos_tt, sin_tt)


def _attn_kernel(lam_ref, g_ref, qt_ref, k_ref, vt_ref, o_ref, sa_ref, sb_ref, *, lambda_init):
    tq = qt_ref.shape[1]
    n_chunks, _, tk = vt_ref.shape
    row = lax.broadcasted_iota(jnp.int32, (LANES, tq), 0)
    ones = jnp.ones((DENOM_ROWS, tk), BF16)
    lam = (jnp.exp(jnp.sum(lam_ref[0:1, :] * lam_ref[1:2, :], axis=-1, keepdims=True))
           - jnp.exp(jnp.sum(lam_ref[2:3, :] * lam_ref[3:4, :], axis=-1, keepdims=True))
           + lambda_init)

    def queries(hd):
        qt = qt_ref[hd * LANES:(hd + 1) * LANES, :]
        zero = jnp.zeros_like(qt)
        return jnp.concatenate([jnp.where(row < QK_DIM, qt, zero),
                                jnp.where(row >= QK_DIM, qt, zero)], axis=1)

    def scores(hd, qqt, j, s_ref):
        start = pl.multiple_of(j * tk, tk)
        kc = k_ref[pl.ds(start, tk), hd * LANES:(hd + 1) * LANES]
        s = jnp.dot(kc, qqt, preferred_element_type=F32)
        s_ref[...] = s
        return jnp.max(s, axis=0, keepdims=True)

    def accumulate(hd, j, s_ref, mx, state):
        m, acc = state
        m_new = jnp.maximum(m, mx)
        alpha = jnp.exp2(m - m_new)
        p = jnp.exp2(s_ref[...] - m_new).astype(BF16)
        va = jnp.concatenate([vt_ref[j, hd * V_DIM:(hd + 1) * V_DIM, :], ones], axis=0)
        acc = alpha * acc + jnp.dot(va, p, preferred_element_type=F32)
        return m_new, acc

    def finalize(hd, acc):
        on = acc[:V_DIM] / acc[V_DIM:V_DIM + 1]
        ot = on[:, :tq] - lam * on[:, tq:]
        ot = ot * lax.rsqrt(jnp.mean(ot * ot, axis=0, keepdims=True) + LN_EPS)
        o_ref[:, hd * V_DIM:(hd + 1) * V_DIM] = (ot.T * g_ref[...] * (1.0 - lambda_init)).astype(BF16)

    n_trips = (n_chunks - 2) // CHUNKS_PER_TRIP
    qqt = queries(0)
    mx_a = scores(0, qqt, 0, sa_ref)
    for hd in range(N_ATTN_HEADS):
        def pair(j, mx_a, state, hd=hd, qqt=qqt):
            mx_b = scores(hd, qqt, j + 1, sb_ref)
            state = accumulate(hd, j, sa_ref, mx_a, state)
            mx_a = scores(hd, qqt, j + 2, sa_ref)
            state = accumulate(hd, j + 1, sb_ref, mx_b, state)
            return mx_a, state

        def body(i, carry, pair=pair):
            for u in range(0, CHUNKS_PER_TRIP, 2):
                carry = pair(CHUNKS_PER_TRIP * i + u, *carry)
            return carry

        state = (jnp.full((1, 2 * tq), -jnp.inf, F32), jnp.zeros((V_DIM + DENOM_ROWS, 2 * tq), F32))
        carry = lax.fori_loop(0, n_trips, body, (mx_a, state))
        for j in range(n_trips * CHUNKS_PER_TRIP, n_chunks - 2, 2):
            carry = pair(j, *carry)
        mx_a, state = carry
        mx_b = scores(hd, qqt, n_chunks - 1, sb_ref)
        state = accumulate(hd, n_chunks - 2, sa_ref, mx_a, state)
        if hd + 1 < N_ATTN_HEADS:
            qqt = queries(hd + 1)
            mx_a = scores(hd + 1, qqt, 0, sa_ref)
        _, acc = accumulate(hd, n_chunks - 1, sb_ref, mx_b, state)
        finalize(hd, acc)


def _attention(qt, k, vt, lam_vecs, subln_g, lambda_init, tq=512):
    b, s, _ = k.shape
    n_chunks, tk = vt.shape[1], vt.shape[3]
    if (n_chunks - 2) // CHUNKS_PER_TRIP == 0:
        tq *= 2
    tq = min(tq, s)
    assert n_chunks % 2 == 0, "the chunk loop is pipelined over pairs"
    once = pl.Buffered(1)
    return pl.pallas_call(
        functools.partial(_attn_kernel, lambda_init=lambda_init),
        grid=(b, s // tq),
        in_specs=[pl.BlockSpec((4, QK_DIM), lambda bi, i: (0, 0)),
                  pl.BlockSpec((1, V_DIM), lambda bi, i: (0, 0)),
                  pl.BlockSpec((None, QK_WIDTH, tq), lambda bi, i: (bi, 0, i)),
                  pl.BlockSpec((None, s, QK_WIDTH), lambda bi, i: (bi, 0, 0), pipeline_mode=once),
                  pl.BlockSpec((None, n_chunks, ATTN_WIDTH, tk), lambda bi, i: (bi, 0, 0, 0),
                               pipeline_mode=once)],
        out_specs=pl.BlockSpec((None, tq, ATTN_WIDTH), lambda bi, i: (bi, i, 0)),
        out_shape=jax.ShapeDtypeStruct((b, s, ATTN_WIDTH), BF16),
        scratch_shapes=[pltpu.VMEM((tk, 2 * tq), F32), pltpu.VMEM((tk, 2 * tq), F32)],
        compiler_params=_params(2),
        name="attention",
    )(lam_vecs, subln_g, qt, k, vt)


def _dft_tables(n):
    idx = np.arange(n, dtype=np.int64)
    ang = (2.0 * np.pi / n) * ((idx[:, None] * idx[None, :]) % n)
    return np.cos(ang), np.sin(ang)


def _channel_table(seq):
    c, s = _dft_tables(FOURIER_GROUP_DIM)
    scale = 1.0 / math.sqrt(seq * FOURIER_GROUP_DIM)
    return jnp.asarray(np.concatenate([c, -s], axis=0) * scale, F32).astype(BF16)


def _fourier_direct_kernel(u_ref, cs_ref, ch_ref, wf_ref, o_ref):
    u = u_ref[...].astype(BF16)
    pr = jnp.dot(cs_ref[0], u, preferred_element_type=F32)
    qi = jnp.dot(cs_ref[1], u, preferred_element_type=F32)
    ch = ch_ref[...]
    for g in range(N_FOURIER_GROUPS):
        sl = slice(g * LANES, (g + 1) * LANES)
        y = jnp.concatenate([pr[:, sl], qi[:, sl]], axis=1).astype(BF16)
        f = jnp.dot(y, ch, preferred_element_type=F32)
        o_ref[:, sl] = jnp.dot(f.astype(BF16), wf_ref[g].astype(BF16),
                               preferred_element_type=F32).astype(o_ref.dtype)


def _fourier_direct(u, w_f, tt=512):
    b, s, _ = u.shape
    tt = min(tt, s)
    c, sn = _dft_tables(s)
    cs = jnp.asarray(np.stack([c, sn]), F32).astype(BF16)
    return pl.pallas_call(
        _fourier_direct_kernel,
        grid=(s // tt, b),
        in_specs=[pl.BlockSpec((None, s, FOURIER_WIDTH), lambda i, bi: (bi, 0, 0)),
                  pl.BlockSpec((2, tt, s), lambda i, bi: (0, i, 0)),
                  pl.BlockSpec((2 * FOURIER_GROUP_DIM, FOURIER_GROUP_DIM), lambda i, bi: (0, 0)),
                  pl.BlockSpec(w_f.shape, lambda i, bi: (0, 0, 0))],
        out_specs=pl.BlockSpec((None, tt, FOURIER_WIDTH), lambda i, bi: (bi, i, 0)),
        out_shape=jax.ShapeDtypeStruct((b, s, FOURIER_WIDTH), BF16),
        compiler_params=_params(2),
        name="fourier_direct",
    )(u, cs, _channel_table(s), w_f)


def _fourier_two_stage_kernel(u_ref, m1_ref, tw_ref, m2_ref, ch_ref, wf_ref, o_ref,
                              er_ref, em_ref, *, n1, n2, unroll1, unroll2):
    m1 = m1_ref[...]
    cb = tw_ref[0]
    sb = tw_ref[1]
    ab = jnp.dot(ch_ref[...], wf_ref[...].astype(BF16), preferred_element_type=F32)
    a_c = ab[:FOURIER_GROUP_DIM]
    nb_c = ab[FOURIER_GROUP_DIM:]
    mch = jnp.concatenate([jnp.concatenate([a_c, -nb_c], axis=1),
                           jnp.concatenate([nb_c, a_c], axis=1)], axis=0).astype(BF16)

    def stage1(s2, carry):
        cw, sw = carry
        us = u_ref[pl.ds(s2, n1, stride=n2), :].astype(BF16)
        gh = jnp.dot(m1, us, preferred_element_type=F32)
        g = gh[:n1]
        h = gh[n1:]
        gt = jnp.concatenate([g * cw - h * sw, g * sw + h * cw], axis=1).astype(BF16)
        e = jnp.dot(gt, mch, preferred_element_type=F32)
        row0 = pl.multiple_of(s2 * n1, n1)
        er_ref[pl.ds(row0, n1), :] = e[:, :LANES]
        em_ref[pl.ds(row0, n1), :] = e[:, LANES:]
        return cw * cb - sw * sb, sw * cb + cw * sb

    lax.fori_loop(0, n2, stage1, (jnp.ones_like(cb), jnp.zeros_like(sb)), unroll=unroll1)

    m2 = m2_ref[...]

    def stage2(t1, carry):
        x = jnp.concatenate([er_ref[pl.ds(t1, n2, stride=n1), :],
                             em_ref[pl.ds(t1, n2, stride=n1), :]], axis=0).astype(BF16)
        o_ref[pl.ds(t1, n2, stride=n1), :] = jnp.dot(m2, x, preferred_element_type=F32)
        return carry

    lax.fori_loop(0, n1, stage2, 0, unroll=unroll2)


def _fourier_two_stage(u, w_f, n1=DFT_STAGE1, unroll1=16, unroll2=8):
    b, s, _ = u.shape
    n2 = s // n1
    c1, s1 = _dft_tables(n1)
    c2, s2 = _dft_tables(n2)
    m1 = jnp.asarray(np.concatenate([c1, s1], axis=0), F32).astype(BF16)
    m2 = jnp.asarray(np.concatenate([c2, -s2], axis=1), F32).astype(BF16)
    ang = (2.0 * np.pi / s) * np.arange(n1, dtype=np.float64)
    tw = np.stack([np.broadcast_to(np.cos(ang)[:, None], (n1, LANES)),
                   np.broadcast_to(np.sin(ang)[:, None], (n1, LANES))])
    tw = jnp.asarray(tw, F32)
    grp = lambda bi, g: (bi, 0, g)
    return pl.pallas_call(
        functools.partial(_fourier_two_stage_kernel, n1=n1, n2=n2,
                          unroll1=min(unroll1, n2), unroll2=min(unroll2, n1)),
        grid=(b, N_FOURIER_GROUPS),
        in_specs=[pl.BlockSpec((None, s, LANES), grp, pipeline_mode=pl.Buffered(1)),
                  pl.BlockSpec(m1.shape, lambda bi, g: (0, 0)),
                  pl.BlockSpec(tw.shape, lambda bi, g: (0, 0, 0)),
                  pl.BlockSpec(m2.shape, lambda bi, g: (0, 0)),
                  pl.BlockSpec((2 * FOURIER_GROUP_DIM, FOURIER_GROUP_DIM), lambda bi, g: (0, 0)),
                  pl.BlockSpec((None, FOURIER_GROUP_DIM, FOURIER_GROUP_DIM), lambda bi, g: (g, 0, 0))],
        out_specs=pl.BlockSpec((None, s, LANES), grp),
        out_shape=jax.ShapeDtypeStruct((b, s, FOURIER_WIDTH), F32),
        scratch_shapes=[pltpu.VMEM((s, LANES), F32), pltpu.VMEM((s, LANES), F32)],
        compiler_params=_params(2),
        name="fourier_two_stage",
    )(u, m1, tw, m2, _channel_table(s), w_f)


def _fourier(u, w_f):
    if u.shape[1] <= DIRECT_DFT_MAX_SEQ:
        return _fourier_direct(u, w_f)
    return _fourier_two_stage(u, w_f)


def _ffn_kernel(x_ref, o_ref, f_ref, mod_ref, wo_ref, ln1g_ref, ln1b_ref,
                wg_ref, wu_ref, wd_ref, ln2g_ref, ln2b_ref, y_ref, *, n_sub):
    g1 = mod_ref[2:3, :]
    sh2 = mod_ref[3:4, :]
    sc2 = mod_ref[4:5, :]
    g2 = mod_ref[5:6, :]
    rows = x_ref.shape[0] // n_sub
    subs = [slice(r * rows, (r + 1) * rows) for r in range(n_sub)]
    hs, ffns = [], []
    for sl in subs:
        mix = (jnp.dot(o_ref[sl, :], wo_ref[:ATTN_WIDTH, :], preferred_element_type=F32)
               + jnp.dot(f_ref[sl, :].astype(BF16), wo_ref[ATTN_WIDTH:, :], preferred_element_type=F32))
        x1 = _layer_norm(ALPHA * x_ref[sl, :] + (1.0 + g1) * mix) * ln1g_ref[...] + ln1b_ref[...]
        y_ref[sl, :] = x1
        hs.append((_layer_norm(x1) * (1.0 + sc2) + sh2).astype(BF16))
    for h in hs:
        gate = jnp.dot(h, wg_ref[...], preferred_element_type=F32)
        up = jnp.dot(h, wu_ref[...], preferred_element_type=F32)
        a = (_silu(gate) * up).astype(BF16)
        ffns.append(jnp.dot(a, wd_ref[...], preferred_element_type=F32))
    for sl, ffn in zip(subs, ffns):
        y_ref[sl, :] = _layer_norm(ALPHA * y_ref[sl, :] + (1.0 + g2) * ffn) * ln2g_ref[...] + ln2b_ref[...]


def _ffn(x, o, f, mod, wo, ln1g, ln1b, wg, wu, wd, ln2g, ln2b, tm=1024, n_sub=4):
    b, s, d = x.shape
    tm = min(tm, s)
    tok = lambda bi, i: (bi, i, 0)
    const = lambda bi, i: (0, 0)
    resident = lambda shape: pl.BlockSpec(shape, const, pipeline_mode=pl.Buffered(1))
    return pl.pallas_call(
        functools.partial(_ffn_kernel, n_sub=n_sub),
        grid=(b, s // tm),
        in_specs=[pl.BlockSpec((None, tm, d), tok),
                  pl.BlockSpec((None, tm, ATTN_WIDTH), tok),
                  pl.BlockSpec((None, tm, FOURIER_WIDTH), tok),
                  pl.BlockSpec((None, N_MOD, d), lambda bi, i: (bi, 0, 0)),
                  resident(wo.shape), resident((1, d)), resident((1, d)),
                  resident(wg.shape), resident(wu.shape), resident(wd.shape),
                  resident((1, d)), resident((1, d))],
        out_specs=pl.BlockSpec((None, tm, d), tok),
        out_shape=jax.ShapeDtypeStruct((b, s, d), F32),
        compiler_params=_params(2),
        name="ffn",
    )(x, o, f, mod, wo, ln1g, ln1b, wg, wu, wd, ln2g, ln2b)


def _rope_tables(seq):
    inv = ROPE_THETA ** (-jnp.arange(0, QK_DIM, 2, dtype=F32) / QK_DIM)
    ang = jnp.arange(seq, dtype=F32)[:, None] * inv[None, :]
    cos = jnp.cos(ang)
    sin = jnp.sin(ang)
    reps = LANES // (QK_DIM // 2)
    sign = jnp.tile(jnp.concatenate([-jnp.ones((QK_DIM // 2,), F32), jnp.ones((QK_DIM // 2,), F32)]),
                    LANES // QK_DIM)
    return jnp.tile(cos, (1, reps)), jnp.tile(sin, (1, reps)) * sign, cos.T, sin.T


def _layer(x, mod, l, w_qv_t, w_ku, lam_vecs, subln_g, w_fourier, w_out, ln1_g, ln1_b,
           w_gate, w_up, w_down, ln2_g, ln2_b):
    lambda_init = 0.8 - 0.6 * math.exp(-0.3 * l)
    s = x.shape[1]
    qt, k, vt, u = _inproj(x, mod, w_qv_t, w_ku, _rope_tables(s), tm=min(KV_CHUNK, s))
    o = _attention(qt, k, vt, lam_vecs, subln_g, lambda_init)
    f = _fourier(u, w_fourier)
    return _ffn(x, o, f, mod, w_out, ln1_g, ln1_b, w_gate, w_up, w_down, ln2_g, ln2_b)


def _split_w_in(w_in):
    q, k, v, u = jnp.split(w_in, [QK_WIDTH, 2 * QK_WIDTH, 2 * QK_WIDTH + ATTN_WIDTH], axis=1)
    return (jnp.concatenate([q, v], axis=1).T.astype(BF16), jnp.concatenate([k, u], axis=1).astype(BF16))


def kernel(x_prompt, x_sample, c_prompt, c_sample, w_ada, b_ada, w_in, lambda_q1, lambda_k1, lambda_q2, lambda_k2, subln_g, w_fourier, w_out, ln1_g, ln1_b, w_gate, w_up, w_down, ln2_g, ln2_b):
    n_prompt = c_prompt.shape[0]
    c_all = jnp.concatenate([c_prompt, c_sample], axis=0)
    y_prompt, y_sample = x_prompt, x_sample
    for l in range(DEPTH):
        mod = _ada(c_all, w_ada[l], b_ada[l][None, :]).reshape(c_all.shape[0], N_MOD, D_MODEL)
        lam_vecs = jnp.stack([lambda_q1[l], lambda_k1[l], lambda_q2[l], lambda_k2[l]]).astype(F32)
        weights = (*_split_w_in(w_in[l]), lam_vecs, subln_g[l][None, :], w_fourier[l],
                   w_out[l].astype(BF16), ln1_g[l][None, :], ln1_b[l][None, :],
                   w_gate[l].astype(BF16), w_up[l].astype(BF16), w_down[l].astype(BF16),
                   ln2_g[l][None, :], ln2_b[l][None, :])
        y_prompt = _layer(y_prompt, mod[:n_prompt], l, *weights)
        y_sample = _layer(y_sample, mod[n_prompt:], l, *weights)
    return (y_prompt, y_sample)
```
